```python
import jax, jax.numpy as jnp
from jax import lax
import numpy as np

D_MODEL = 1024
BATCH = 1
SEQ = 16384
DEPTH = 1
DEC_BATCH = 128
DEC_SEQ = 1
PAST_LEN = 16384
PAGE_SIZE = 128

H_A = 8
Q_LORA = D_MODEL // 4
KV_LORA = D_MODEL // 4
NOPE_DIM = 64
ROPE_DIM = 32
V_DIM = 64
W_A = H_A * V_DIM
H_B = 8
DH_B = 64
W_B = H_B * DH_B
Q_BLOCK = 128
ROPE_THETA = 10000.0
EPS = 1e-6
FORGET_BIAS = 3.0
MLA_SCALE = (NOPE_DIM + ROPE_DIM) ** -0.5
FOX_SCALE = DH_B ** -0.5
SPLIT_SIZES = (Q_LORA, KV_LORA, ROPE_DIM, W_A, W_B, W_B, W_B, H_B, W_B, D_MODEL, D_MODEL)
D_IN = Q_LORA + KV_LORA + ROPE_DIM + W_A + 4 * W_B + H_B + 2 * D_MODEL

kernel_name = 'hybrid_mla_fox_gated_decoder_step'


def rms_norm(x, g):
    xf = x.astype(jnp.float32)
    y = xf * lax.rsqrt(jnp.mean(xf * xf, axis=-1, keepdims=True) + EPS)
    return (y * g.astype(jnp.float32)).astype(x.dtype)


def rope_angles(pos):
    inv = ROPE_THETA ** (-jnp.arange(0, ROPE_DIM, 2, dtype=jnp.float32) / ROPE_DIM)
    ang = pos.astype(jnp.float32)[:, None] * inv[None, :]
    return jnp.cos(ang), jnp.sin(ang)


def apply_rope(x, cos, sin):
    xf = x.astype(jnp.float32)
    x1, x2 = jnp.split(xf, 2, axis=-1)
    return jnp.concatenate([x1 * cos - x2 * sin, x2 * cos + x1 * sin], axis=-1).astype(x.dtype)


def masked_softmax(s, mask):
    return jax.nn.softmax(jnp.where(mask, s, -jnp.inf), axis=-1)


def branch_inputs(h, pos, w_in, g_q, w_uq, g_kv, b_f):
    b, t = h.shape[0], h.shape[1]
    z = jnp.einsum('btd,de->bte', h, w_in)
    offs = [int(o) for o in np.cumsum(SPLIT_SIZES)[:-1]]
    c_q, c_kv, k_r, gate_a, q_b, k_b, v_b, f_pre, gate_b, gl_a, gl_b = jnp.split(z, offs, axis=-1)
    cos, sin = rope_angles(pos)
    q = jnp.einsum('btc,che->bthe', rms_norm(c_q, g_q), w_uq)
    q_nope = q[..., :NOPE_DIM]
    q_rope = apply_rope(q[..., NOPE_DIM:], cos[:, None, :], sin[:, None, :])
    lat = rms_norm(c_kv, g_kv)
    k_rope = apply_rope(k_r, cos, sin)
    heads = lambda a: a.reshape(b, t, H_B, DH_B)
    logf = jax.nn.log_sigmoid((f_pre + b_f).astype(jnp.float32))
    return q_nope, q_rope, lat, k_rope, heads(q_b), heads(k_b), heads(v_b), logf, (gate_a, gate_b, gl_a, gl_b)


def merge_branches(x, o_a, o_b, gates, w_pa, w_pb, w_out, g_post):
    gate_a, gate_b, gl_a, gl_b = gates
    b, t = x.shape[0], x.shape[1]
    u_a = o_a.reshape(b, t, W_A) * jax.nn.silu(gate_a)
    u_b = o_b.reshape(b, t, W_B) * jax.nn.silu(gate_b)
    m = jax.nn.sigmoid(gl_a) * (u_a @ w_pa) + jax.nn.sigmoid(gl_b) * (u_b @ w_pb)
    return x + rms_norm(m @ w_out, g_post)


def prompt_attention(q_nope, q_rope, lat, k_rope, w_uk, w_uv, q_b, k_b, v_b, logf):
    b, s = q_nope.shape[0], q_nope.shape[1]
    k_nope = jnp.einsum('bsc,chn->bshn', lat, w_uk)
    v_a = jnp.einsum('bsc,chv->bshv', lat, w_uv)
    cum_h = jnp.transpose(jnp.cumsum(logf.astype(jnp.float32), axis=1), (0, 2, 1))
    key_pos = jnp.arange(s)

    def block(i):
        start = i * Q_BLOCK
        sl = lambda a: lax.dynamic_slice_in_dim(a, start, Q_BLOCK, axis=1)
        causal = key_pos[None, :] <= (start + jnp.arange(Q_BLOCK))[:, None]
        s_a = (jnp.einsum('bqhn,bshn->bhqs', sl(q_nope), k_nope)
               + jnp.einsum('bqhr,bsr->bhqs', sl(q_rope), k_rope)).astype(jnp.float32) * MLA_SCALE
        p_a = masked_softmax(s_a, causal)
        o_a = jnp.einsum('bhqs,bshv->bqhv', p_a.astype(v_a.dtype), v_a)
        cq = lax.dynamic_slice_in_dim(cum_h, start, Q_BLOCK, axis=2)
        decay = cq[..., None] - cum_h[:, :, None, :]
        s_b = jnp.einsum('bqhd,bshd->bhqs', sl(q_b), k_b).astype(jnp.float32) * FOX_SCALE + decay
        p_b = masked_softmax(s_b, causal)
        o_b = jnp.einsum('bhqs,bshd->bqhd', p_b.astype(v_b.dtype), v_b)
        return o_a, o_b

    o_a, o_b = lax.map(block, jnp.arange(s // Q_BLOCK))
    o_a = jnp.moveaxis(o_a, 0, 1).reshape(b, s, H_A, V_DIM)
    o_b = jnp.moveaxis(o_b, 0, 1).reshape(b, s, H_B, DH_B)
    return o_a, o_b


def sample_attention(q_lat, q_rope, lat_new, kr_new, q_b, k_new, v_new, lf_new, page_table,
                     c_lat, c_kr, c_k, c_v, c_lf):
    t = q_lat.shape[1]
    total = PAST_LEN + t
    causal = jnp.arange(total)[None, :] <= (PAST_LEN + jnp.arange(t))[:, None]

    def gather(cache, pages):
        g = cache[pages]
        return g.reshape((-1,) + g.shape[2:])

    def one(args):
        ql, qr, ln, krn, qb, kn, vn, lfn, pages = args
        lat = jnp.concatenate([gather(c_lat, pages), ln], axis=0)
        kr = jnp.concatenate([gather(c_kr, pages), krn], axis=0)
        s_a = (jnp.einsum('thc,sc->hts', ql, lat)
               + jnp.einsum('thr,sr->hts', qr, kr)).astype(jnp.float32) * MLA_SCALE
        p_a = masked_softmax(s_a, causal)
        o_lat = jnp.einsum('hts,sc->thc', p_a.astype(lat.dtype), lat)
        k = jnp.concatenate([gather(c_k, pages), kn], axis=0)
        v = jnp.concatenate([gather(c_v, pages), vn], axis=0)
        lf = jnp.concatenate([gather(c_lf, pages).astype(jnp.float32), lfn.astype(jnp.float32)], axis=0)
        cum = jnp.cumsum(lf, axis=0)
        decay = cum[PAST_LEN:].T[:, :, None] - cum.T[:, None, :]
        s_b = jnp.einsum('thd,shd->hts', qb, k).astype(jnp.float32) * FOX_SCALE + decay
        p_b = masked_softmax(s_b, causal)
        o_b = jnp.einsum('hts,shd->thd', p_b.astype(v.dtype), v)
        return o_lat, o_b

    return lax.map(one, (q_lat, q_rope, lat_new, kr_new, q_b, k_new, v_new, lf_new, page_table))


def layer(x_p, x_s, page_table, c_lat, c_kr, c_k, c_v, c_lf,
          g_pre, w_in, g_q, w_uq, g_kv, w_uk, w_uv, b_f, w_pa, w_pb, w_out, g_post):
    qn, qr, lat, kr, qb, kb, vb, lf, gates = branch_inputs(
        rms_norm(x_p, g_pre), jnp.arange(x_p.shape[1]), w_in, g_q, w_uq, g_kv, b_f)
    o_a, o_b = prompt_attention(qn, qr, lat, kr, w_uk, w_uv, qb, kb, vb, lf)
    y_p = merge_branches(x_p, o_a, o_b, gates, w_pa, w_pb, w_out, g_post)
    qn_s, qr_s, lat_s, kr_s, qb_s, kb_s, vb_s, lf_s, gates_s = branch_inputs(
        rms_norm(x_s, g_pre), PAST_LEN + jnp.arange(x_s.shape[1]), w_in, g_q, w_uq, g_kv, b_f)
    q_lat = jnp.einsum('bthn,chn->bthc', qn_s, w_uk)
    o_lat, o_b_s = sample_attention(q_lat, qr_s, lat_s, kr_s, qb_s, kb_s, vb_s, lf_s, page_table,
                                    c_lat, c_kr, c_k, c_v, c_lf)
    o_a_s = jnp.einsum('bthc,chv->bthv', o_lat, w_uv)
    y_s = merge_branches(x_s, o_a_s, o_b_s, gates_s, w_pa, w_pb, w_out, g_post)
    return y_p, y_s, (lat, kr, kb, vb, lf), (lat_s, kr_s, kb_s, vb_s, lf_s)


def setup_inputs(seed: int = 0) -> dict:
    key = jax.random.key(seed)
    ks = jax.random.split(key, 24)
    n_pages = PAST_LEN // PAGE_SIZE
    n_used = DEC_BATCH * n_pages
    n_pool = (n_used * 5 + 3) // 4
    nrm = lambda k, shape, scale=1.0: jax.random.normal(k, shape, dtype=jnp.float32) * scale
    page_table = jax.random.permutation(ks[0], n_pool)[:n_used].reshape(DEC_BATCH, n_pages).astype(jnp.int32)
    return {
        'x_prompt': nrm(ks[1], (BATCH, SEQ, D_MODEL)),
        'x_sample': nrm(ks[2], (DEC_BATCH, DEC_SEQ, D_MODEL)),
        'cache_mla_latent': nrm(ks[3], (DEPTH, n_pool, PAGE_SIZE, KV_LORA)),
        'cache_mla_krope': nrm(ks[4], (DEPTH, n_pool, PAGE_SIZE, ROPE_DIM)),
        'cache_fox_k': nrm(ks[5], (DEPTH, n_pool, PAGE_SIZE, H_B, DH_B)),
        'cache_fox_v': nrm(ks[6], (DEPTH, n_pool, PAGE_SIZE, H_B, DH_B)),
        'cache_fox_logf': jax.nn.log_sigmoid(FORGET_BIAS + nrm(ks[7], (DEPTH, n_pool, PAGE_SIZE, H_B))),
        'page_table': page_table,
        'g_pre': 1.0 + nrm(ks[8], (DEPTH, D_MODEL), 0.02),
        'w_in': nrm(ks[9], (DEPTH, D_MODEL, D_IN), D_MODEL ** -0.5),
        'g_q': 1.0 + nrm(ks[10], (DEPTH, Q_LORA), 0.02),
        'w_uq': nrm(ks[11], (DEPTH, Q_LORA, H_A, NOPE_DIM + ROPE_DIM), Q_LORA ** -0.5),
        'g_kv': 1.0 + nrm(ks[12], (DEPTH, KV_LORA), 0.02),
        'w_uk': nrm(ks[13], (DEPTH, KV_LORA, H_A, NOPE_DIM), KV_LORA ** -0.5),
        'w_uv': nrm(ks[14], (DEPTH, KV_LORA, H_A, V_DIM), KV_LORA ** -0.5),
        'b_f': FORGET_BIAS + nrm(ks[15], (DEPTH, H_B), 0.1),
        'w_pa': nrm(ks[16], (DEPTH, W_A, D_MODEL), W_A ** -0.5),
        'w_pb': nrm(ks[17], (DEPTH, W_B, D_MODEL), W_B ** -0.5),
        'w_out': nrm(ks[18], (DEPTH, D_MODEL, D_MODEL), D_MODEL ** -0.5),
        'g_post': 1.0 + nrm(ks[19], (DEPTH, D_MODEL), 0.02),
    }


def reference(x_prompt, x_sample, cache_mla_latent, cache_mla_krope, cache_fox_k, cache_fox_v, cache_fox_logf,
              page_table, g_pre, w_in, g_q, w_uq, g_kv, w_uk, w_uv, b_f, w_pa, w_pb, w_out, g_post):
    x_p, x_s = x_prompt, x_sample
    st_p = [[], [], [], [], []]
    st_s = [[], [], [], [], []]
    for l in range(DEPTH):
        x_p, x_s, new_p, new_s = layer(
            x_p, x_s, page_table,
            cache_mla_latent[l], cache_mla_krope[l], cache_fox_k[l], cache_fox_v[l], cache_fox_logf[l],
            g_pre[l], w_in[l], g_q[l], w_uq[l], g_kv[l], w_uk[l], w_uv[l], b_f[l],
            w_pa[l], w_pb[l], w_out[l], g_post[l])
        for lst, a in zip(st_p, new_p):
            lst.append(a)
        for lst, a in zip(st_s, new_s):
            lst.append(a)
    return (x_p, x_s,
            jnp.stack(st_p[0]), jnp.stack(st_p[1]), jnp.stack(st_p[2]), jnp.stack(st_p[3]), jnp.stack(st_p[4]),
            jnp.stack(st_s[0]), jnp.stack(st_s[1]), jnp.stack(st_s[2]), jnp.stack(st_s[3]), jnp.stack(st_s[4]))
```

```python
import functools

import numpy as np
import jax
import jax.numpy as jnp
from jax import lax
from jax.experimental import pallas as pl
from jax.experimental.pallas import tpu as pltpu

D_MODEL = 1024
N_HEADS = 8
Q_LORA = 256
KV_LORA = 256
NOPE_DIM = 64
ROPE_DIM = 32
V_DIM = 64
DH_B = 64
W_A = N_HEADS * V_DIM
W_B = N_HEADS * DH_B
PAGE_SIZE = 128
ROPE_THETA = 10000.0
EPS = 1e-6
MLA_SCALE = (NOPE_DIM + ROPE_DIM) ** -0.5
FOX_SCALE = DH_B ** -0.5
LANES = 128
HEAD_PAD = 128
N_QA = N_HEADS * HEAD_PAD

_C_CQ = 0
_C_CKV = _C_CQ + Q_LORA
_C_QB = _C_CKV + KV_LORA
_C_VB = _C_QB + W_B
_C_GATES = _C_VB + W_B
_N_GATES = W_A + W_B + 2 * D_MODEL
_N_W1 = _C_GATES + _N_GATES
_R_KB = 0
_R_VB = _R_KB + W_B
_R_CKV = _R_VB + W_B
_R_KR = _R_CKV + KV_LORA
_R_F = _R_KR + 2 * LANES
_N_WT = _R_F + 16

VMEM_LIMIT = 56 * 1024 * 1024

BF16 = jnp.bfloat16
F32 = jnp.float32


def _dot(a, b):
    return jnp.dot(a, b, preferred_element_type=F32)


def _dot_nt(a, b):
    return lax.dot_general(a, b, (((1,), (1,)), ((), ())), preferred_element_type=F32)


def _rms(x, g):
    return x * lax.rsqrt(jnp.mean(x * x, axis=-1, keepdims=True) + EPS) * g


def _sigmoid(x):
    return 1.0 / (1.0 + jnp.exp(-x))


def _split3(x):
    x1 = x.astype(BF16).astype(F32)
    r1 = x - x1
    x2 = r1.astype(BF16).astype(F32)
    x3 = (r1 - x2).astype(BF16).astype(F32)
    return x1, x2, x3


def _project_kernel(x_ref, gpre_ref, w1_ref, wt_ref, gq_ref, wq2_ref, gkv_ref, gkvc_ref, wuv_ref, wukt_ref, er_ref,
                    bf_ref, u_ref, tab_ref, tabt_ref,
                    lat_ref, krt_ref, kbt_ref, vbt_ref, lf_ref, ck_ref, qa_ref, kat_ref, va_ref, qb_ref,
                    kbt16_ref, vb16_ref, g_ref, carry_ref):
    i = pl.program_id(0)

    @pl.when(i == 0)
    def _():
        carry_ref[...] = jnp.zeros_like(carry_ref)

    x = x_ref[...]
    h = _rms(x, gpre_ref[...]).astype(BF16)

    cq = _dot(h, w1_ref[:, _C_CQ:_C_CQ + Q_LORA])
    cqn = _rms(cq, gq_ref[...]).astype(BF16)
    q2 = _dot(cqn, wq2_ref[...])
    ct = tab_ref[:, 0:LANES]
    st = tab_ref[:, LANES:]
    for hd in range(N_HEADS):
        lo = hd * HEAD_PAD
        qa_ref[:, lo:lo + HEAD_PAD] = (q2[:, lo:lo + HEAD_PAD] * ct
                                       + q2[:, N_QA + lo:N_QA + lo + HEAD_PAD] * st).astype(BF16)

    ckv = _dot(h, w1_ref[:, _C_CKV:_C_CKV + KV_LORA])
    lat = _rms(ckv, gkv_ref[...])
    lat_ref[...] = lat
    va_ref[...] = _dot(lat.astype(BF16), wuv_ref[...]).astype(BF16)

    zt = _dot_nt(wt_ref[...], h)
    kbt = zt[_R_KB:_R_KB + W_B]
    kbt_ref[...] = kbt
    kbt16_ref[...] = kbt.astype(BF16)
    vbt_ref[...] = zt[_R_VB:_R_VB + W_B]
    ckvt = zt[_R_CKV:_R_CKV + KV_LORA]
    latt = ckvt * lax.rsqrt(jnp.mean(ckvt * ckvt, axis=0, keepdims=True) + EPS) * gkvc_ref[...]
    krt = (zt[_R_KR:_R_KR + LANES] * tabt_ref[0:LANES, :]
           + zt[_R_KR + LANES:_R_KR + 2 * LANES] * tabt_ref[LANES:, :])
    krt_ref[...] = krt[0:ROPE_DIM]
    kat_ref[...] = (_dot(wukt_ref[...], latt.astype(BF16)) + _dot(er_ref[...], krt.astype(BF16))).astype(BF16)

    qb_ref[...] = _dot(h, w1_ref[:, _C_QB:_C_QB + W_B]).astype(BF16)
    vb16_ref[...] = _dot(h, w1_ref[:, _C_VB:_C_VB + W_B]).astype(BF16)
    g_ref[...] = _dot(h, w1_ref[:, _C_GATES:_C_GATES + _N_GATES])

    f_t = zt[_R_F:_R_F + N_HEADS] + bf_ref[...]
    lf = jnp.minimum(f_t, 0.0) - jnp.log(1.0 + jnp.exp(-jnp.abs(f_t)))
    lf_ref[...] = lf
    parts = jnp.concatenate(_split3(lf), axis=0).astype(BF16)
    cs = _dot(parts, u_ref[...])
    cum = cs[0:N_HEADS] + cs[N_HEADS:2 * N_HEADS] + cs[2 * N_HEADS:] + carry_ref[:, 0:1]
    ck_ref[...] = cum
    bm = cum.shape[1]
    carry_ref[...] = jnp.broadcast_to(cum[:, bm - 1:bm], carry_ref.shape)


def _project(x, pw, tab, tabt, bm):
    n = x.shape[0]
    assert n % bm == 0
    grid = (n // bm,)
    row = lambda w: pl.BlockSpec((bm, w), lambda i: (i, 0))
    col = lambda r: pl.BlockSpec((r, bm), lambda i: (0, i))
    const = lambda a: pl.BlockSpec(a.shape, lambda i: (0,) * a.ndim, pipeline_mode=pl.Buffered(1))
    u = (jnp.arange(bm)[:, None] <= jnp.arange(bm)[None, :]).astype(BF16)
    consts = (pw["g_pre"], pw["w1"], pw["wt"], pw["g_q"], pw["wq2"], pw["g_kv"], pw["g_kv_col"], pw["wuv"],
              pw["wukt"], pw["er"], pw["b_f"], u)
    out_shape = (
        jax.ShapeDtypeStruct((n, KV_LORA), F32),
        jax.ShapeDtypeStruct((ROPE_DIM, n), F32),
        jax.ShapeDtypeStruct((W_B, n), F32),
        jax.ShapeDtypeStruct((W_B, n), F32),
        jax.ShapeDtypeStruct((N_HEADS, n), F32),
        jax.ShapeDtypeStruct((N_HEADS, n), F32),
        jax.ShapeDtypeStruct((n, N_QA), BF16),
        jax.ShapeDtypeStruct((N_QA, n), BF16),
        jax.ShapeDtypeStruct((n, W_A), BF16),
        jax.ShapeDtypeStruct((n, W_B), BF16),
        jax.ShapeDtypeStruct((W_B, n), BF16),
        jax.ShapeDtypeStruct((n, W_B), BF16),
        jax.ShapeDtypeStruct((n, _N_GATES), F32),
    )
    out_specs = (row(KV_LORA), col(ROPE_DIM), col(W_B), col(W_B), col(N_HEADS), col(N_HEADS),
                 row(N_QA), col(N_QA), row(W_A), row(W_B), col(W_B), row(W_B), row(_N_GATES))
    return pl.pallas_call(
        _project_kernel,
        grid=grid,
        in_specs=[row(D_MODEL)] + [const(a) for a in consts] + [row(2 * LANES), col(2 * LANES)],
        out_specs=out_specs,
        out_shape=out_shape,
        scratch_shapes=[pltpu.VMEM((N_HEADS, LANES), F32)],
        compiler_params=pltpu.CompilerParams(dimension_semantics=("arbitrary",), vmem_limit_bytes=VMEM_LIMIT),
        name="project",
    )(x, *consts, tab, tabt)


def _flash_kernel(qi_ref, ki_ref, qa_ref, kat_ref, va_ref, qb_ref, kbt_ref, vb_ref, ck_ref, oa_ref, ob_ref,
                  m_ref, l_ref, acc_ref, qbm_ref):
    t = pl.program_id(1)
    qi = qi_ref[t]
    ki = ki_ref[t]
    bq = qa_ref.shape[0]
    bk = kat_ref.shape[1]
    half = lax.broadcasted_iota(jnp.int32, (bq, LANES), 1) < V_DIM

    @pl.when(ki == 0)
    def _():
        m_ref[...] = jnp.full_like(m_ref, -jnp.inf)
        l_ref[...] = jnp.zeros_like(l_ref)
        acc_ref[...] = jnp.zeros_like(acc_ref)
        qb = qb_ref[...]
        qbm_ref[0] = jnp.where(half, qb, jnp.zeros_like(qb))
        qbm_ref[1] = jnp.where(half, jnp.zeros_like(qb), qb)

    def step(masked):
        if masked:
            keep = (lax.broadcasted_iota(jnp.int32, (bq, bk), 1)
                    <= lax.broadcasted_iota(jnp.int32, (bq, bk), 0))
        for hh in range(4):
            if hh < 2:
                q = qa_ref[:, hh * HEAD_PAD:(hh + 1) * HEAD_PAD]
                k = kat_ref[hh * HEAD_PAD:(hh + 1) * HEAD_PAD, :]
                v = va_ref[...]
            else:
                q = qbm_ref[hh - 2]
                k = kbt_ref[...]
                v = vb_ref[...]
            s = _dot(q, k)
            if hh >= 2:
                s = s - ck_ref[hh - 2:hh - 1, :]
            if masked:
                s = jnp.where(keep, s, -jnp.inf)
            m_prev = m_ref[hh]
            m_new = jnp.maximum(m_prev, jnp.max(s, axis=1, keepdims=True))
            alpha = jnp.exp(m_prev - m_new)
            p = jnp.exp(s - m_new[:, 0:1])
            l_ref[hh] = alpha * l_ref[hh] + jnp.sum(p, axis=1, keepdims=True)
            acc_ref[hh] = alpha * acc_ref[hh] + _dot(p.astype(BF16), v)
            m_ref[hh] = m_new

    @pl.when(ki < qi)
    def _():
        step(False)

    @pl.when(ki == qi)
    def _():
        step(True)
        oa_ref[...] = jnp.where(half, acc_ref[0] / l_ref[0], acc_ref[1] / l_ref[1])
        ob_ref[...] = jnp.where(half, acc_ref[2] / l_ref[2], acc_ref[3] / l_ref[3])


def _flash_prompt(qa, kat, va, qb, kbt, vb, ck, blk):
    n = qa.shape[0]
    assert n % blk == 0
    nq = n // blk
    qi_tab = np.concatenate([np.full(i + 1, i, np.int32) for i in range(nq)])
    ki_tab = np.concatenate([np.arange(i + 1, dtype=np.int32) for i in range(nq)])
    n_pairs = N_HEADS // 2
    qmap = lambda w: pl.BlockSpec((blk, w), lambda p, t, qi, ki: (qi[t], p))
    vmap = lambda w: pl.BlockSpec((blk, w), lambda p, t, qi, ki: (ki[t], p))
    kmap = lambda r: pl.BlockSpec((r, blk), lambda p, t, qi, ki: (p, ki[t]))
    grid_spec = pltpu.PrefetchScalarGridSpec(
        num_scalar_prefetch=2,
        grid=(n_pairs, qi_tab.shape[0]),
        in_specs=[qmap(2 * HEAD_PAD), kmap(2 * HEAD_PAD), vmap(LANES), qmap(LANES), kmap(LANES), vmap(LANES),
                  pl.BlockSpec((None, 2, blk), lambda p, t, qi, ki: (p, 0, ki[t]))],
        out_specs=(qmap(LANES), qmap(LANES)),
        scratch_shapes=[pltpu.VMEM((4, blk, LANES), F32), pltpu.VMEM((4, blk, LANES), F32),
                        pltpu.VMEM((4, blk, LANES), F32), pltpu.VMEM((2, blk, LANES), BF16)],
    )
    return pl.pallas_call(
        _flash_kernel,
        grid_spec=grid_spec,
        out_shape=(jax.ShapeDtypeStruct((n, W_A), F32), jax.ShapeDtypeStruct((n, W_B), F32)),
        compiler_params=pltpu.CompilerParams(dimension_semantics=("arbitrary", "arbitrary"),
                                             vmem_limit_bytes=VMEM_LIMIT),
        name="flash_prompt",
    )(jnp.asarray(qi_tab), jnp.asarray(ki_tab), qa, kat, va, qb, kbt, vb, ck.reshape(n_pairs, 2, n))


def _merge_kernel(x_ref, oa_ref, ob_ref, g_ref, wpa_ref, wpb_ref, wout_ref, gpost_ref, y_ref):
    ga = g_ref[:, 0:W_A]
    gb = g_ref[:, W_A:W_A + W_B]
    gla = g_ref[:, W_A + W_B:W_A + W_B + D_MODEL]
    glb = g_ref[:, W_A + W_B + D_MODEL:]
    ua = (oa_ref[...] * (ga * _sigmoid(ga))).astype(BF16)
    ub = (ob_ref[...] * (gb * _sigmoid(gb))).astype(BF16)
    m = _sigmoid(gla) * _dot(ua, wpa_ref[...]) + _sigmoid(glb) * _dot(ub, wpb_ref[...])
    mo = _dot(m.astype(BF16), wout_ref[...])
    y_ref[...] = x_ref[...] + _rms(mo, gpost_ref[...])


def _merge(x, oa, ob, gates, pw, bm):
    n = x.shape[0]
    assert n % bm == 0
    row = lambda w: pl.BlockSpec((bm, w), lambda i: (i, 0))
    const = lambda a: pl.BlockSpec(a.shape, lambda i: (0,) * a.ndim)
    consts = (pw["w_pa"], pw["w_pb"], pw["w_out"], pw["g_post"])
    return pl.pallas_call(
        _merge_kernel,
        grid=(n // bm,),
        in_specs=[row(D_MODEL), row(W_A), row(W_B), row(_N_GATES)] + [const(a) for a in consts],
        out_specs=row(D_MODEL),
        out_shape=jax.ShapeDtypeStruct((n, D_MODEL), F32),
        compiler_params=pltpu.CompilerParams(dimension_semantics=("parallel",), vmem_limit_bytes=VMEM_LIMIT),
        name="merge",
    )(x, oa, ob, gates, *consts)


def _bf(x):
    return x.astype(BF16).astype(F32)


def _decode_kernel(pt_ref, qa_ref, qb_ref, latn_ref, krn_ref, kn_ref, vn_ref, lfn_ref, wukt_ref, er_ref, wuv_ref,
                   tt_ref, *rest, pages_per_step):
    pps = pages_per_step
    lat_refs = rest[0:pps]
    kr_refs = rest[pps:2 * pps]
    k_refs = rest[2 * pps:3 * pps]
    v_refs = rest[3 * pps:4 * pps]
    lf_refs = rest[4 * pps:5 * pps]
    oa_ref, ob_ref = rest[5 * pps:5 * pps + 2]
    (qlat_ref, qrope_ref, qbd_ref, ma_ref, la_ref, mb_ref, lb_ref, acca_ref, accb_ref, carry_ref,
     pb_ref, alpha_ref) = rest[5 * pps + 2:]
    b = pl.program_id(0)
    j = pl.program_id(1)

    def head_mask(width, group):
        lane = lax.broadcasted_iota(jnp.int32, (N_HEADS, width), 1)
        sub = lax.broadcasted_iota(jnp.int32, (N_HEADS, width), 0)
        return (lane // group) == sub

    @pl.when(j == 0)
    def _():
        mine = lax.broadcasted_iota(jnp.int32, (1, LANES), 1) == lax.rem(b, LANES)

        def pick(x8):
            return jnp.sum(jnp.where(mine, x8, 0.0), axis=1, keepdims=True)

        qa = jnp.broadcast_to(qa_ref[...].astype(F32), (N_HEADS, N_QA))
        qa = jnp.where(head_mask(N_QA, HEAD_PAD), qa, 0.0).astype(BF16)
        qlat = _dot(qa, wukt_ref[...])
        qrope = _dot(qa, er_ref[...])
        qb = jnp.broadcast_to(qb_ref[...].astype(F32), (N_HEADS, W_B))
        qbd = jnp.where(head_mask(W_B, DH_B), qb, 0.0)
        qlat_ref[...] = qlat
        qrope_ref[...] = qrope
        qbd_ref[...] = qbd
        latn = _bf(latn_ref[...])
        sa = (jnp.sum(_bf(qlat) * latn, axis=1, keepdims=True)
              + pick(_dot(qrope[:, 0:ROPE_DIM].astype(BF16), krn_ref[...].astype(BF16))))
        sb = pick(_dot(qbd.astype(BF16), kn_ref[...].astype(BF16)))
        ma_ref[...] = jnp.broadcast_to(sa, ma_ref.shape)
        mb_ref[...] = jnp.broadcast_to(sb, mb_ref.shape)
        la_ref[...] = jnp.ones_like(la_ref)
        lb_ref[...] = jnp.ones_like(lb_ref)
        acca_ref[...] = jnp.broadcast_to(latn, acca_ref.shape)
        accb_ref[...] = jnp.where(mine, vn_ref[...], 0.0)
        carry_ref[...] = jnp.broadcast_to(pick(lfn_ref[...]), carry_ref.shape)

    qlat = qlat_ref[...].astype(BF16)
    qrope = qrope_ref[:, 0:ROPE_DIM].astype(BF16)
    qbd = qbd_ref[...].astype(BF16)
    lats = [r[...].astype(BF16) for r in lat_refs]
    carry = carry_ref[...]
    sa_parts, sb_parts = [], []
    for i in range(pps):
        parts = jnp.concatenate(_split3(lf_refs[i][...]), axis=0).astype(BF16)
        y3 = _dot(parts, tt_ref[...])
        y = y3[0:N_HEADS] + y3[N_HEADS:2 * N_HEADS] + y3[2 * N_HEADS:]
        sb_parts.append(_dot(qbd, k_refs[i][...].astype(BF16)) + y[:, 0:PAGE_SIZE] + carry)
        carry = carry + y[:, PAGE_SIZE:]
        sa_parts.append(_dot_nt(qlat, lats[i]) + _dot(qrope, kr_refs[i][...].astype(BF16)))
    carry_ref[...] = carry
    sa = jnp.concatenate(sa_parts, axis=1)
    sb = jnp.concatenate(sb_parts, axis=1)

    def update(s, m_ref, l_ref):
        m_prev = m_ref[...]
        m_new = jnp.maximum(m_prev, jnp.max(s, axis=1, keepdims=True))
        alpha = jnp.exp(m_prev - m_new)
        p = jnp.exp(s - m_new[:, 0:1])
        l_ref[...] = alpha * l_ref[...] + jnp.sum(p, axis=1, keepdims=True)
        m_ref[...] = m_new
        return alpha, p

    alpha_a, pa = update(sa, ma_ref, la_ref)
    alpha_b, pb = update(sb, mb_ref, lb_ref)
    pa = pa.astype(BF16)
    da = _dot(pa[:, 0:PAGE_SIZE], lats[0])
    for i in range(1, pps):
        da = da + _dot(pa[:, i * PAGE_SIZE:(i + 1) * PAGE_SIZE], lats[i])
    acca_ref[...] = alpha_a[:, 0:1] * acca_ref[...] + da

    pb_ref[...] = pb
    alpha_ref[...] = alpha_b
    for hd in range(N_HEADS):
        rows = slice(hd * DH_B, (hd + 1) * DH_B)
        acc = accb_ref[rows, :] * alpha_ref[hd:hd + 1, :]
        for i in range(pps):
            acc = acc + v_refs[i][rows, :] * pb_ref[hd:hd + 1, i * PAGE_SIZE:(i + 1) * PAGE_SIZE]
        accb_ref[rows, :] = acc

    @pl.when(j == pl.num_programs(1) - 1)
    def _():
        olat = (acca_ref[...] / la_ref[:, 0:1]).astype(BF16)
        oa8 = _dot(olat, wuv_ref[...])
        diag = head_mask(W_A, V_DIM)
        oa_ref[...] = jnp.sum(jnp.where(diag, oa8, 0.0), axis=0, keepdims=True)
        ones = jnp.ones((N_HEADS, LANES), BF16)
        ob8 = sum(_dot_nt(ones, part.astype(BF16)) for part in _split3(accb_ref[...]))
        ob_ref[...] = jnp.sum(jnp.where(diag, ob8 / lb_ref[:, 0:1], 0.0), axis=0, keepdims=True)


def _decode_sample(page_table, qa, qb, lat_new, krt_new, kbt_new, vbt_new, lft_new, pw, c_lat, c_krt, c_kt, c_vt,
                   c_lft, pages_per_step):
    nb, n_pages = page_table.shape
    pps = pages_per_step
    assert n_pages % pps == 0 and nb % LANES == 0
    per_b = lambda w: pl.BlockSpec((None, 1, w), lambda b, j, pt: (b, 0, 0))
    lane_b = lambda r: pl.BlockSpec((r, LANES), lambda b, j, pt: (0, b // LANES))
    const = lambda a: pl.BlockSpec(a.shape, lambda b, j, pt: (0,) * a.ndim)

    def page(r, w, i):
        return pl.BlockSpec((None, r, w), lambda b, j, pt: (pt[b, n_pages - 1 - (j * pps + i)], 0, 0))

    pos = np.arange(PAGE_SIZE)
    tt = jnp.asarray(np.concatenate([pos[:, None] > pos[None, :], np.ones((PAGE_SIZE, PAGE_SIZE), bool)], axis=1),
                     BF16)
    consts = (pw["wukt"], pw["er"], pw["wuv"], tt)
    in_specs = ([per_b(N_QA), per_b(W_B), per_b(KV_LORA), lane_b(ROPE_DIM), lane_b(W_B), lane_b(W_B),
                 lane_b(N_HEADS)]
                + [const(a) for a in consts]
                + [page(PAGE_SIZE, KV_LORA, i) for i in range(pps)]
                + [page(ROPE_DIM, PAGE_SIZE, i) for i in range(pps)]
                + [page(W_B, PAGE_SIZE, i) for i in range(pps)]
                + [page(W_B, PAGE_SIZE, i) for i in range(pps)]
                + [page(N_HEADS, PAGE_SIZE, i) for i in range(pps)])
    small = lambda w: pltpu.VMEM((N_HEADS, w), F32)
    grid_spec = pltpu.PrefetchScalarGridSpec(
        num_scalar_prefetch=1,
        grid=(nb, n_pages // pps),
        in_specs=in_specs,
        out_specs=(per_b(W_A), per_b(W_B)),
        scratch_shapes=[small(KV_LORA), small(LANES), small(W_B),
                        small(LANES), small(LANES), small(LANES), small(LANES),
                        small(KV_LORA), pltpu.VMEM((W_B, LANES), F32), small(LANES),
                        small(pps * PAGE_SIZE), small(LANES)],
    )
    r3 = lambda a: a.reshape(nb, 1, a.shape[-1])
    oa, ob = pl.pallas_call(
        functools.partial(_decode_kernel, pages_per_step=pps),
        grid_spec=grid_spec,
        out_shape=(jax.ShapeDtypeStruct((nb, 1, W_A), F32), jax.ShapeDtypeStruct((nb, 1, W_B), F32)),
        compiler_params=pltpu.CompilerParams(dimension_semantics=("arbitrary", "arbitrary"),
                                             vmem_limit_bytes=VMEM_LIMIT),
        name="decode_sample",
    )(page_table, r3(qa), r3(qb), r3(lat_new), krt_new, kbt_new, vbt_new, lft_new, *consts,
      *([c_lat] * pps), *([c_krt] * pps), *([c_kt] * pps), *([c_vt] * pps), *([c_lft] * pps))
    return oa.reshape(nb, W_A), ob.reshape(nb, W_B)


def _prep_weights(g_pre, w_in, g_q, w_uq, g_kv, w_uk, w_uv, b_f, w_pa, w_pb, w_out, g_post):
    offs = np.cumsum([0, Q_LORA, KV_LORA, ROPE_DIM, W_A, W_B, W_B, W_B, N_HEADS, W_B, D_MODEL, D_MODEL])
    seg = lambda k: w_in[:, offs[k]:offs[k + 1]]
    half = ROPE_DIM // 2
    swap = lambda w: jnp.concatenate([w[..., half:], w[..., :half]], axis=-1)
    pad_to = lambda w, n: jnp.pad(w, [(0, 0)] * (w.ndim - 1) + [(0, n - w.shape[-1])])
    w_kr = seg(2)
    w1 = jnp.concatenate([seg(0), seg(1), seg(4) * FOX_SCALE, seg(6), seg(3), seg(8), seg(9), seg(10)], axis=1)
    assert w1.shape[1] == _N_W1
    wt = jnp.concatenate([seg(5), seg(6), seg(1), pad_to(w_kr, LANES), pad_to(swap(w_kr), LANES),
                          pad_to(seg(7), 16)], axis=1).T
    assert wt.shape[0] == _N_WT
    uq_n, uq_r = w_uq[..., :NOPE_DIM], w_uq[..., NOPE_DIM:]
    zpad = HEAD_PAD - NOPE_DIM - ROPE_DIM
    wq_plain = jnp.concatenate([uq_n, uq_r, jnp.zeros(uq_n.shape[:2] + (zpad,), F32)], axis=-1)
    wq_swap = jnp.concatenate([jnp.zeros_like(uq_n), swap(uq_r), jnp.zeros(uq_n.shape[:2] + (zpad,), F32)], axis=-1)
    wq2 = jnp.concatenate([wq_plain.reshape(Q_LORA, -1), wq_swap.reshape(Q_LORA, -1)], axis=1)
    wk_pad = pad_to(w_uk, HEAD_PAD)
    wukt = jnp.transpose(wk_pad, (1, 2, 0)).reshape(N_QA, KV_LORA)
    r = np.arange(ROPE_DIM)
    er = np.zeros((N_QA, LANES), np.float32)
    for hd in range(N_HEADS):
        er[hd * HEAD_PAD + NOPE_DIM + r, r] = 1.0
    return {
        "g_pre": g_pre.reshape(1, -1), "w1": w1.astype(BF16), "wt": wt.astype(BF16),
        "g_q": g_q.reshape(1, -1), "wq2": wq2.astype(BF16),
        "g_kv": g_kv.reshape(1, -1), "g_kv_col": g_kv.reshape(-1, 1), "b_f": b_f.reshape(N_HEADS, 1),
        "w_pa": w_pa.astype(BF16), "w_pb": w_pb.astype(BF16), "w_out": w_out.astype(BF16),
        "g_post": g_post.reshape(1, -1),
        "wukt": wukt.astype(BF16), "er": jnp.asarray(er, BF16), "wuv": w_uv.reshape(KV_LORA, W_A).astype(BF16),
    }


def _rope_tables(pos):
    inv = ROPE_THETA ** (-jnp.arange(0, ROPE_DIM, 2, dtype=F32) / ROPE_DIM)
    ang = pos.astype(F32)[:, None] * inv[None, :]
    cos, sin = jnp.cos(ang), jnp.sin(ang)
    cos2 = jnp.concatenate([cos, cos], axis=1)
    sin2 = jnp.concatenate([-sin, sin], axis=1)
    t = pos.shape[0]
    z = lambda n: jnp.zeros((t, n), F32)
    zpad = HEAD_PAD - NOPE_DIM - ROPE_DIM
    ct = jnp.concatenate([jnp.ones((t, NOPE_DIM), F32), cos2, z(zpad)], axis=1) * MLA_SCALE
    st = jnp.concatenate([z(NOPE_DIM), sin2, z(zpad)], axis=1) * MLA_SCALE
    kt = jnp.concatenate([cos2, z(LANES - ROPE_DIM), sin2, z(LANES - ROPE_DIM)], axis=1).T
    return jnp.concatenate([ct, st], axis=1), kt


def _pick(n, prefs):
    for b in prefs:
        if n % b == 0:
            return b
    return n


def _layer(x_p, x_s, page_table, c_lat, c_kr, c_k, c_v, c_lf, pw):
    bp, sp, _ = x_p.shape
    bs, ss, _ = x_s.shape
    assert bp == 1 and ss == 1
    n_pool = c_lat.shape[0]
    past = page_table.shape[1] * PAGE_SIZE
    xp = x_p.reshape(sp, D_MODEL)
    xs = x_s.reshape(bs, D_MODEL)

    (lat, krt, kbt, vbt, lf_t, ck_t, qa, kat, va, qb, kbt16, vb16, gates) = _project(
        xp, pw, *_rope_tables(jnp.arange(sp)), _pick(sp, (512, 256, 128)))
    oa, ob = _flash_prompt(qa, kat, va, qb, kbt16, vb16, ck_t, _pick(sp, (512, 256, 128)))
    y_p = _merge(xp, oa, ob, gates, pw, _pick(sp, (512, 256, 128)))

    (lat_s, krt_s, kbt_s, vbt_s, lf_ts, _, qa_s, _, _, qb_s, _, _, gates_s) = _project(
        xs, pw, *_rope_tables(past + jnp.zeros((bs,), jnp.int32)), _pick(bs, (512, 256, 128)))
    c_krt = jnp.transpose(c_kr, (0, 2, 1))
    c_kt = jnp.transpose(c_k, (0, 2, 3, 1)).reshape(n_pool, W_B, PAGE_SIZE)
    c_vt = jnp.transpose(c_v, (0, 2, 3, 1)).reshape(n_pool, W_B, PAGE_SIZE)
    c_lft = jnp.transpose(c_lf, (0, 2, 1))
    oa_s, ob_s = _decode_sample(page_table, qa_s, qb_s, lat_s, krt_s, kbt_s, vbt_s, lf_ts, pw,
                                c_lat, c_krt, c_kt, c_vt, c_lft, _pick(page_table.shape[1], (8, 4, 2)))
    y_s = _merge(xs, oa_s, ob_s, gates_s, pw, _pick(bs, (512, 256, 128)))

    heads = lambda a, n: jnp.transpose(a.reshape(N_HEADS, DH_B, n), (2, 0, 1))
    new_p = (lat.reshape(bp, sp, KV_LORA), krt.T.reshape(bp, sp, ROPE_DIM), heads(kbt, sp).reshape(bp, sp, N_HEADS, DH_B),
             heads(vbt, sp).reshape(bp, sp, N_HEADS, DH_B), lf_t.T.reshape(bp, sp, N_HEADS))
    new_s = (lat_s.reshape(bs, ss, KV_LORA), krt_s.T.reshape(bs, ss, ROPE_DIM),
             heads(kbt_s, bs).reshape(bs, ss, N_HEADS, DH_B), heads(vbt_s, bs).reshape(bs, ss, N_HEADS, DH_B),
             lf_ts.T.reshape(bs, ss, N_HEADS))
    return y_p.reshape(bp, sp, D_MODEL), y_s.reshape(bs, ss, D_MODEL), new_p, new_s


def kernel(x_prompt, x_sample, cache_mla_latent, cache_mla_krope, cache_fox_k, cache_fox_v, cache_fox_logf,
           page_table, g_pre, w_in, g_q, w_uq, g_kv, w_uk, w_uv, b_f, w_pa, w_pb, w_out, g_post):
    depth = w_in.shape[0]
    x_p, x_s = x_prompt, x_sample
    st_p = [[] for _ in range(5)]
    st_s = [[] for _ in range(5)]
    for l in range(depth):
        pw = _prep_weights(g_pre[l], w_in[l], g_q[l], w_uq[l], g_kv[l], w_uk[l], w_uv[l], b_f[l],
                           w_pa[l], w_pb[l], w_out[l], g_post[l])
        lay = lambda a: a.reshape(a.shape[1:]) if depth == 1 else a[l]
        x_p, x_s, new_p, new_s = _layer(x_p, x_s, page_table, lay(cache_mla_latent), lay(cache_mla_krope),
                                        lay(cache_fox_k), lay(cache_fox_v), lay(cache_fox_logf), pw)
        for lst, a in zip(st_p, new_p):
            lst.append(a)
        for lst, a in zip(st_s, new_s):
            lst.append(a)
    return (x_p, x_s) + tuple(jnp.stack(s) for s in st_p) + tuple(jnp.stack(s) for s in st_s)
```

```python
import functools
import math

import numpy as np
import jax
import jax.numpy as jnp
from jax import lax
from jax.experimental import pallas as pl
from jax.experimental.pallas import tpu as pltpu

D_MODEL = 1024
N_HEADS = 8
Q_LORA = 256
KV_LORA = 256
NOPE_DIM = 64
ROPE_DIM = 32
V_DIM = 64
DH_B = 64
W_A = N_HEADS * V_DIM
W_B = N_HEADS * DH_B
PAGE_SIZE = 128
ROPE_THETA = 10000.0
EPS = 1e-6
MLA_SCALE = (NOPE_DIM + ROPE_DIM) ** -0.5
FOX_SCALE = DH_B ** -0.5
LOG2E = math.log2(math.e)
LANES = 128
HEAD_PAD = 128
N_QA = N_HEADS * HEAD_PAD
N_SPLIT = 3
ONE_EVEN = V_DIM
ONE_ODD = 0

_C_CQ = 0
_C_CKV = _C_CQ + Q_LORA
_C_QB = _C_CKV + KV_LORA
_C_VB = _C_QB + N_QA
_C_GATES = _C_VB + N_QA
_N_GATES = W_A + W_B + 2 * D_MODEL
_N_W1 = _C_GATES + _N_GATES
_R_KB = 0
_R_VB = _R_KB + N_QA
_R_CKV = _R_VB + W_B
_R_KR = _R_CKV + KV_LORA
_R_F = _R_KR + 2 * LANES
_N_WT = _R_F + 16
_N_PARTS = 32

VMEM_LIMIT = 56 * 1024 * 1024
FLASH_BQ = 1024
FLASH_BK = 512
DECODE_PAGES = 16

BF16 = jnp.bfloat16
F32 = jnp.float32


def _dot(a, b):
    return jnp.dot(a, b, preferred_element_type=F32)


def _dot_nt(a, b):
    return lax.dot_general(a, b, (((1,), (1,)), ((), ())), preferred_element_type=F32)


def _rms(x, g):
    return x * lax.rsqrt(jnp.mean(x * x, axis=-1, keepdims=True) + EPS) * g


def _sigmoid(x):
    return 1.0 / (1.0 + jnp.exp(-x))


def _split3(x):
    x1 = x.astype(BF16).astype(F32)
    r1 = x - x1
    x2 = r1.astype(BF16).astype(F32)
    x3 = (r1 - x2).astype(BF16).astype(F32)
    return x1, x2, x3


def _project_kernel(x_ref, gpre_ref, w1_ref, wt_ref, gq_ref, wq2_ref, gkv_ref, gkvc_ref, wuvp_ref, wukt_ref, er_ref,
                    bf_ref, u_ref, pc_ref, patq_ref, patv_ref, tab_ref, tabt_ref,
                    lat_ref, krt_ref, kbt_ref, vbt_ref, lf_ref, qa_ref, kat_ref, va_ref, qb_ref,
                    kbt16_ref, vb16_ref, g_ref, carry_ref, *, qb_scale, cum_scale):
    i = pl.program_id(0)

    @pl.when(i == 0)
    def _():
        carry_ref[...] = jnp.zeros_like(carry_ref)

    x = x_ref[...]
    h = _rms(x, gpre_ref[...]).astype(BF16)

    cq = _dot(h, w1_ref[:, _C_CQ:_C_CQ + Q_LORA])
    cqn = _rms(cq, gq_ref[...]).astype(BF16)
    q2 = _dot(cqn, wq2_ref[...])
    ct = tab_ref[:, 0:LANES]
    st = tab_ref[:, LANES:]
    for hd in range(N_HEADS):
        lo = hd * HEAD_PAD
        qa_ref[:, lo:lo + HEAD_PAD] = (q2[:, lo:lo + HEAD_PAD] * ct
                                       + q2[:, N_QA + lo:N_QA + lo + HEAD_PAD] * st).astype(BF16)

    ckv = _dot(h, w1_ref[:, _C_CKV:_C_CKV + KV_LORA])
    lat = _rms(ckv, gkv_ref[...])
    lat_ref[...] = lat
    va_ref[...] = (_dot(lat.astype(BF16), wuvp_ref[...]) + patv_ref[...]).astype(BF16)

    zt = _dot_nt(wt_ref[...], h)
    kbz = zt[_R_KB:_R_KB + N_QA]
    for hd in range(N_HEADS):
        kbt_ref[hd * DH_B:(hd + 1) * DH_B, :] = kbz[hd * HEAD_PAD:hd * HEAD_PAD + DH_B]
    vbt_ref[...] = zt[_R_VB:_R_VB + W_B]
    ckvt = zt[_R_CKV:_R_CKV + KV_LORA]
    latt = ckvt * lax.rsqrt(jnp.mean(ckvt * ckvt, axis=0, keepdims=True) + EPS) * gkvc_ref[...]
    krt = (zt[_R_KR:_R_KR + LANES] * tabt_ref[0:LANES, :]
           + zt[_R_KR + LANES:_R_KR + 2 * LANES] * tabt_ref[LANES:, :])
    krt_ref[...] = krt[0:ROPE_DIM]
    kat_ref[...] = (_dot(wukt_ref[...], latt.astype(BF16)) + _dot(er_ref[...], krt.astype(BF16))).astype(BF16)

    qb_ref[...] = (_dot(h, w1_ref[:, _C_QB:_C_QB + N_QA]) * qb_scale + patq_ref[...]).astype(BF16)
    vb16_ref[...] = (_dot(h, w1_ref[:, _C_VB:_C_VB + N_QA]) + patv_ref[...]).astype(BF16)
    g_ref[...] = _dot(h, w1_ref[:, _C_GATES:_C_GATES + _N_GATES])

    f_t = zt[_R_F:_R_F + N_HEADS] + bf_ref[...]
    lf = jnp.minimum(f_t, 0.0) - jnp.log(1.0 + jnp.exp(-jnp.abs(f_t)))
    lf_ref[...] = lf
    parts = jnp.concatenate(_split3(lf), axis=0).astype(BF16)
    cs = _dot(parts, u_ref[...])
    cum = cs[0:N_HEADS] + cs[N_HEADS:2 * N_HEADS] + cs[2 * N_HEADS:] + carry_ref[:, 0:1]
    bm = cum.shape[1]
    carry_ref[...] = jnp.broadcast_to(cum[:, bm - 1:bm], carry_ref.shape)
    cparts = jnp.concatenate(_split3(cum * cum_scale) + (jnp.zeros_like(cum),), axis=0).astype(BF16)
    kbt16_ref[...] = (kbz + _dot(pc_ref[...], cparts)).astype(BF16)


def _project(x, pw, tab, tabt, bm, qb_scale, cum_scale):
    n = x.shape[0]
    assert n % bm == 0
    grid = (n // bm,)
    row = lambda w: pl.BlockSpec((bm, w), lambda i: (i, 0))
    col = lambda r: pl.BlockSpec((r, bm), lambda i: (0, i))
    const = lambda a: pl.BlockSpec(a.shape, lambda i: (0,) * a.ndim, pipeline_mode=pl.Buffered(1))
    u = (jnp.arange(bm)[:, None] <= jnp.arange(bm)[None, :]).astype(BF16)
    consts = (pw["g_pre"], pw["w1"], pw["wt"], pw["g_q"], pw["wq2"], pw["g_kv"], pw["g_kv_col"], pw["wuvp"],
              pw["wukt"], pw["er"], pw["b_f"], u, pw["pc"], pw["patq"], pw["patv"])
    out_shape = (
        jax.ShapeDtypeStruct((n, KV_LORA), F32),
        jax.ShapeDtypeStruct((ROPE_DIM, n), F32),
        jax.ShapeDtypeStruct((W_B, n), F32),
        jax.ShapeDtypeStruct((W_B, n), F32),
        jax.ShapeDtypeStruct((N_HEADS, n), F32),
        jax.ShapeDtypeStruct((n, N_QA), BF16),
        jax.ShapeDtypeStruct((N_QA, n), BF16),
        jax.ShapeDtypeStruct((n, N_QA), BF16),
        jax.ShapeDtypeStruct((n, N_QA), BF16),
        jax.ShapeDtypeStruct((N_QA, n), BF16),
        jax.ShapeDtypeStruct((n, N_QA), BF16),
        jax.ShapeDtypeStruct((n, _N_GATES), F32),
    )
    out_specs = (row(KV_LORA), col(ROPE_DIM), col(W_B), col(W_B), col(N_HEADS),
                 row(N_QA), col(N_QA), row(N_QA), row(N_QA), col(N_QA), row(N_QA), row(_N_GATES))
    return pl.pallas_call(
        functools.partial(_project_kernel, qb_scale=qb_scale, cum_scale=cum_scale),
        grid=grid,
        in_specs=[row(D_MODEL)] + [const(a) for a in consts] + [row(2 * LANES), col(2 * LANES)],
        out_specs=out_specs,
        out_shape=out_shape,
        scratch_shapes=[pltpu.VMEM((N_HEADS, LANES), F32)],
        compiler_params=pltpu.CompilerParams(dimension_semantics=("arbitrary",), vmem_limit_bytes=VMEM_LIMIT),
        name="project",
    )(x, *consts, tab, tabt)


def _flash_kernel(qa_ref, kat_ref, va_ref, qb_ref, kbt_ref, vb_ref, oa_ref, ob_ref, m_ref, acc_ref, *, bk):
    qi = pl.program_id(1)
    blk = qa_ref.shape[0]
    sub = blk // bk
    m_ref[...] = jnp.full_like(m_ref, -jnp.inf)
    acc_ref[...] = jnp.zeros_like(acc_ref)

    def block(k0, diag):
        if diag is not None:
            keep = (lax.broadcasted_iota(jnp.int32, (blk, bk), 1) + diag * bk
                    <= lax.broadcasted_iota(jnp.int32, (blk, bk), 0))
        for hh in range(4):
            q_ref, k_ref, v_ref = (qa_ref, kat_ref, va_ref) if hh < 2 else (qb_ref, kbt_ref, vb_ref)
            lo = (hh % 2) * HEAD_PAD
            s = _dot(q_ref[:, lo:lo + HEAD_PAD], k_ref[lo:lo + HEAD_PAD, pl.ds(k0, bk)])
            if diag is not None:
                s = jnp.where(keep, s, -jnp.inf)
            m_prev = m_ref[hh]
            m_new = jnp.maximum(m_prev, jnp.max(s, axis=1, keepdims=True))
            alpha = jnp.exp2(m_prev - m_new)
            p = jnp.exp2(s - m_new[:, 0:1]).astype(BF16)
            acc_ref[hh] = alpha * acc_ref[hh] + _dot(p, v_ref[pl.ds(k0, bk), lo:lo + HEAD_PAD])
            m_ref[hh] = m_new

    def unmasked(ki, c):
        block(pl.multiple_of(ki * bk, bk), None)
        return c

    lax.fori_loop(0, qi * sub, unmasked, 0)
    for t in range(sub):
        block(pl.multiple_of(qi * blk + t * bk, bk), t)

    half = lax.broadcasted_iota(jnp.int32, (blk, LANES), 1) < V_DIM

    def pair_out(a_even, a_odd):
        return jnp.where(half, a_even / a_even[:, ONE_EVEN:ONE_EVEN + 1], a_odd / a_odd[:, ONE_ODD:ONE_ODD + 1])

    oa_ref[...] = pair_out(acc_ref[0], acc_ref[1])
    ob_ref[...] = pair_out(acc_ref[2], acc_ref[3])


def _flash_prompt(qa, kat, va, qb, kbt, vb, blk, bk):
    n = qa.shape[0]
    assert n % blk == 0 and blk % bk == 0
    n_pairs = N_HEADS // 2
    qmap = pl.BlockSpec((blk, 2 * HEAD_PAD), lambda p, i: (i, p))
    kmap = pl.BlockSpec((2 * HEAD_PAD, n), lambda p, i: (p, 0), pipeline_mode=pl.Buffered(1))
    vmap = pl.BlockSpec((n, 2 * HEAD_PAD), lambda p, i: (0, p), pipeline_mode=pl.Buffered(1))
    omap = pl.BlockSpec((blk, LANES), lambda p, i: (i, p))
    return pl.pallas_call(
        functools.partial(_flash_kernel, bk=bk),
        grid=(n_pairs, n // blk),
        in_specs=[qmap, kmap, vmap, qmap, kmap, vmap],
        out_specs=(omap, omap),
        out_shape=(jax.ShapeDtypeStruct((n, W_A), F32), jax.ShapeDtypeStruct((n, W_B), F32)),
        scratch_shapes=[pltpu.VMEM((4, blk, LANES), F32), pltpu.VMEM((4, blk, LANES), F32)],
        compiler_params=pltpu.CompilerParams(dimension_semantics=("arbitrary", "arbitrary"),
                                             vmem_limit_bytes=VMEM_LIMIT),
        name="flash_prompt",
    )(qa, kat, va, qb, kbt, vb)


def _merge_kernel(x_ref, oa_ref, ob_ref, g_ref, wpa_ref, wpb_ref, wout_ref, gpost_ref, y_ref):
    ga = g_ref[:, 0:W_A]
    gb = g_ref[:, W_A:W_A + W_B]
    gla = g_ref[:, W_A + W_B:W_A + W_B + D_MODEL]
    glb = g_ref[:, W_A + W_B + D_MODEL:]
    ua = (oa_ref[...] * (ga * _sigmoid(ga))).astype(BF16)
    ub = (ob_ref[...] * (gb * _sigmoid(gb))).astype(BF16)
    m = _sigmoid(gla) * _dot(ua, wpa_ref[...]) + _sigmoid(glb) * _dot(ub, wpb_ref[...])
    mo = _dot(m.astype(BF16), wout_ref[...])
    y_ref[...] = x_ref[...] + _rms(mo, gpost_ref[...])


def _merge(x, oa, ob, gates, pw, bm):
    n = x.shape[0]
    assert n % bm == 0
    row = lambda w: pl.BlockSpec((bm, w), lambda i: (i, 0))
    const = lambda a: pl.BlockSpec(a.shape, lambda i: (0,) * a.ndim)
    consts = (pw["w_pa"], pw["w_pb"], pw["w_out"], pw["g_post"])
    return pl.pallas_call(
        _merge_kernel,
        grid=(n // bm,),
        in_specs=[row(D_MODEL), row(W_A), row(W_B), row(_N_GATES)] + [const(a) for a in consts],
        out_specs=row(D_MODEL),
        out_shape=jax.ShapeDtypeStruct((n, D_MODEL), F32),
        compiler_params=pltpu.CompilerParams(dimension_semantics=("parallel",), vmem_limit_bytes=VMEM_LIMIT),
        name="merge",
    )(x, oa, ob, gates, *consts)


def _bf(x):
    return x.astype(BF16).astype(F32)


def _decode_kernel(pt_ref, qa_ref, qb_ref, latn_ref, krn_ref, kn_ref, vn_ref, lfn_ref, wukt_ref, er_ref, cmp_ref,
                   wuv_ref, tt_ref, clat_hbm, ckr_hbm, ck_hbm, cv_hbm, clf_hbm, oa_ref, ob_ref,
                   lat_buf, kr_buf, k_buf, v_buf, lf_buf, sem,
                   qlat_ref, qrope_ref, qbd_ref, ma_ref, la_ref, mb_ref, lb_ref, acca_ref, accb_ref, carry_ref,
                   pb_ref, alpha_ref, *, pages_per_step):
    pps = pages_per_step
    b = pl.program_id(0)
    nb = pl.num_programs(0)
    n_pages = pt_ref.shape[1]
    n_steps = n_pages // pps
    n_pairs = n_steps // 2
    caches = (clat_hbm, ckr_hbm, ck_hbm, cv_hbm, clf_hbm)
    bufs = (lat_buf, kr_buf, k_buf, v_buf, lf_buf)

    def page_copy(a, page, slot, i):
        return pltpu.make_async_copy(caches[a].at[page], bufs[a].at[slot, i], sem.at[slot, a])

    def start_step(bb, jj, slot):
        for i in range(pps):
            page = pt_ref[bb, n_pages - 1 - (jj * pps + i)]
            for a in range(len(caches)):
                page_copy(a, page, slot, i).start()

    def wait_step(slot):
        for i in range(pps):
            for a in range(len(caches)):
                page_copy(a, 0, slot, i).wait()

    def head_mask(width, group):
        lane = lax.broadcasted_iota(jnp.int32, (N_HEADS, width), 1)
        sub = lax.broadcasted_iota(jnp.int32, (N_HEADS, width), 0)
        return (lane // group) == sub

    @pl.when(b == 0)
    def _():
        start_step(0, 0, 0)

    def init_state():
        mine = lax.broadcasted_iota(jnp.int32, (1, LANES), 1) == lax.rem(b, LANES)

        def pick(x8):
            return jnp.sum(jnp.where(mine, x8, 0.0), axis=1, keepdims=True)

        groups = head_mask(N_QA, HEAD_PAD)
        qa = jnp.broadcast_to(qa_ref[...].astype(F32), (N_HEADS, N_QA))
        qa = jnp.where(groups, qa, 0.0).astype(BF16)
        qlat = _dot(qa, wukt_ref[...])
        qrope = _dot(qa, er_ref[...])
        qb = jnp.broadcast_to(qb_ref[...].astype(F32), (N_HEADS, N_QA))
        qbd = _dot(jnp.where(groups, qb, 0.0).astype(BF16), cmp_ref[...])
        qlat_ref[...] = qlat
        qrope_ref[...] = qrope
        qbd_ref[...] = qbd
        latn = _bf(latn_ref[...])
        sa = (jnp.sum(_bf(qlat) * latn, axis=1, keepdims=True)
              + pick(_dot(qrope[:, 0:ROPE_DIM].astype(BF16), krn_ref[...].astype(BF16))))
        sb = pick(_dot(qbd.astype(BF16), kn_ref[...].astype(BF16)))
        ma_ref[...] = jnp.broadcast_to(sa, ma_ref.shape)
        mb_ref[...] = jnp.broadcast_to(sb, mb_ref.shape)
        la_ref[...] = jnp.ones_like(la_ref)
        lb_ref[...] = jnp.ones_like(lb_ref)
        acca_ref[...] = jnp.broadcast_to(latn, acca_ref.shape)
        accb_ref[...] = jnp.where(mine, vn_ref[...], 0.0)
        carry_ref[...] = jnp.broadcast_to(pick(lfn_ref[...]), carry_ref.shape)

    def update(s, m_ref, l_ref):
        m_prev = m_ref[...]
        m_new = jnp.maximum(m_prev, jnp.max(s, axis=1, keepdims=True))
        alpha = jnp.exp(m_prev - m_new)
        p = jnp.exp(s - m_new[:, 0:1])
        l_ref[...] = alpha * l_ref[...] + jnp.sum(p, axis=1, keepdims=True)
        m_ref[...] = m_new
        return alpha, p

    def attend(slot):
        qlat = qlat_ref[...].astype(BF16)
        qrope = qrope_ref[:, 0:ROPE_DIM].astype(BF16)
        qbd = qbd_ref[...].astype(BF16)
        lats = [lat_buf[slot, i].astype(BF16) for i in range(pps)]
        carry = carry_ref[...]
        sa_parts, sb_parts = [], []
        for i in range(pps):
            parts = jnp.concatenate(_split3(lf_buf[slot, i]), axis=0).astype(BF16)
            y3 = _dot(parts, tt_ref[...])
            y = y3[0:N_HEADS] + y3[N_HEADS:2 * N_HEADS] + y3[2 * N_HEADS:]
            sb_parts.append(_dot(qbd, k_buf[slot, i].astype(BF16)) + y[:, 0:PAGE_SIZE] + carry)
            carry = carry + y[:, PAGE_SIZE:]
            sa_parts.append(_dot_nt(qlat, lats[i]) + _dot(qrope, kr_buf[slot, i].astype(BF16)))
        carry_ref[...] = carry
        alpha_a, pa = update(jnp.concatenate(sa_parts, axis=1), ma_ref, la_ref)
        alpha_b, pb = update(jnp.concatenate(sb_parts, axis=1), mb_ref, lb_ref)
        pa = pa.astype(BF16)
        da = _dot(pa[:, 0:PAGE_SIZE], lats[0])
        for i in range(1, pps):
            da = da + _dot(pa[:, i * PAGE_SIZE:(i + 1) * PAGE_SIZE], lats[i])
        acca_ref[...] = alpha_a[:, 0:1] * acca_ref[...] + da
        pb_ref[...] = pb
        alpha_ref[...] = alpha_b
        for hd in range(N_HEADS):
            rows = slice(hd * DH_B, (hd + 1) * DH_B)
            acc = accb_ref[rows, :] * alpha_ref[hd:hd + 1, :]
            for i in range(pps):
                acc = acc + v_buf[slot, i, rows, :] * pb_ref[hd:hd + 1, i * PAGE_SIZE:(i + 1) * PAGE_SIZE]
            accb_ref[rows, :] = acc

    init_state()

    def two_steps(pair, c):
        j0 = 2 * pair
        start_step(b, j0 + 1, 1)
        wait_step(0)
        attend(0)
        last = pair == n_pairs - 1
        nxt_b = jnp.where(last, jnp.minimum(b + 1, nb - 1), b)
        nxt_j = jnp.where(last, jnp.where(b + 1 < nb, 0, n_steps - 1), j0 + 2)
        start_step(nxt_b, nxt_j, 0)
        wait_step(1)
        attend(1)
        return c

    lax.fori_loop(0, n_pairs, two_steps, 0)

    @pl.when(b == nb - 1)
    def _():
        wait_step(0)

    olat = (acca_ref[...] / la_ref[:, 0:1]).astype(BF16)
    oa8 = _dot(olat, wuv_ref[...])
    diag = head_mask(W_A, V_DIM)
    oa_ref[...] = jnp.sum(jnp.where(diag, oa8, 0.0), axis=0, keepdims=True)
    ones = jnp.ones((N_HEADS, LANES), BF16)
    ob8 = sum(_dot_nt(ones, part.astype(BF16)) for part in _split3(accb_ref[...]))
    ob_ref[...] = jnp.sum(jnp.where(diag, ob8 / lb_ref[:, 0:1], 0.0), axis=0, keepdims=True)


def _decode_sample(page_table, qa, qb, lat_new, krt_new, kbt_new, vbt_new, lft_new, pw, c_lat, c_krt, c_kt, c_vt,
                   c_lft, pages_per_step):
    nb, n_pages = page_table.shape
    pps = pages_per_step
    assert n_pages % (2 * pps) == 0 and nb % LANES == 0
    per_b = lambda w: pl.BlockSpec((None, 1, w), lambda b, pt: (b, 0, 0))
    lane_b = lambda r: pl.BlockSpec((r, LANES), lambda b, pt: (0, b // LANES))
    const = lambda a: pl.BlockSpec(a.shape, lambda b, pt: (0,) * a.ndim)
    in_hbm = pl.BlockSpec(memory_space=pl.ANY)
    pos = np.arange(PAGE_SIZE)
    tt = jnp.asarray(np.concatenate([pos[:, None] > pos[None, :], np.ones((PAGE_SIZE, PAGE_SIZE), bool)], axis=1),
                     BF16)
    consts = (pw["wukt"], pw["er"], pw["cmp"], pw["wuv"], tt)
    caches = (c_lat, c_krt, c_kt, c_vt, c_lft)
    small = lambda w: pltpu.VMEM((N_HEADS, w), F32)
    grid_spec = pltpu.PrefetchScalarGridSpec(
        num_scalar_prefetch=1,
        grid=(nb,),
        in_specs=([per_b(N_QA), per_b(N_QA), per_b(KV_LORA), lane_b(ROPE_DIM), lane_b(W_B), lane_b(W_B),
                   lane_b(N_HEADS)] + [const(a) for a in consts] + [in_hbm] * len(caches)),
        out_specs=(per_b(W_A), per_b(W_B)),
        scratch_shapes=([pltpu.VMEM((2, pps) + c.shape[1:], F32) for c in caches]
                        + [pltpu.SemaphoreType.DMA((2, len(caches)))]
                        + [small(KV_LORA), small(LANES), small(W_B),
                           small(LANES), small(LANES), small(LANES), small(LANES),
                           small(KV_LORA), pltpu.VMEM((W_B, LANES), F32), small(LANES),
                           small(pps * PAGE_SIZE), small(LANES)]),
    )
    r3 = lambda a: a.reshape(nb, 1, a.shape[-1])
    oa, ob = pl.pallas_call(
        functools.partial(_decode_kernel, pages_per_step=pps),
        grid_spec=grid_spec,
        out_shape=(jax.ShapeDtypeStruct((nb, 1, W_A), F32), jax.ShapeDtypeStruct((nb, 1, W_B), F32)),
        compiler_params=pltpu.CompilerParams(dimension_semantics=("arbitrary",), vmem_limit_bytes=VMEM_LIMIT),
        name="decode_sample",
    )(page_table, r3(qa), r3(qb), r3(lat_new), krt_new, kbt_new, vbt_new, lft_new, *consts, *caches)
    return oa.reshape(nb, W_A), ob.reshape(nb, W_B)


def _head_groups(w, offset=0):
    lead = w.shape[:-1]
    w = w.reshape(lead + (N_HEADS, -1))
    width = w.shape[-1]
    pads = [(0, 0)] * (w.ndim - 1) + [(offset, HEAD_PAD - width - offset)]
    return jnp.pad(w, pads).reshape(lead + (N_QA,))


def _prep_weights(g_pre, w_in, g_q, w_uq, g_kv, w_uk, w_uv, b_f, w_pa, w_pb, w_out, g_post):
    offs = np.cumsum([0, Q_LORA, KV_LORA, ROPE_DIM, W_A, W_B, W_B, W_B, N_HEADS, W_B, D_MODEL, D_MODEL])
    seg = lambda k: w_in[:, offs[k]:offs[k + 1]]
    half = ROPE_DIM // 2
    swap = lambda w: jnp.concatenate([w[..., half:], w[..., :half]], axis=-1)
    pad_to = lambda w, n: jnp.pad(w, [(0, 0)] * (w.ndim - 1) + [(0, n - w.shape[-1])])
    even = (np.arange(N_HEADS) % 2 == 0)

    def value_groups(w):
        lo, hi = _head_groups(w, 0), _head_groups(w, HEAD_PAD - V_DIM)
        sel = jnp.asarray(np.repeat(even, HEAD_PAD))
        return jnp.where(sel, lo, hi)

    w_kr = seg(2)
    w1 = jnp.concatenate([seg(0), seg(1), _head_groups(seg(4)), value_groups(seg(6)),
                          seg(3), seg(8), seg(9), seg(10)], axis=1)
    assert w1.shape[1] == _N_W1
    wt = jnp.concatenate([_head_groups(seg(5)), seg(6), seg(1), pad_to(w_kr, LANES), pad_to(swap(w_kr), LANES),
                          pad_to(seg(7), 16)], axis=1).T
    assert wt.shape[0] == _N_WT
    uq_n, uq_r = w_uq[..., :NOPE_DIM], w_uq[..., NOPE_DIM:]
    zpad = HEAD_PAD - NOPE_DIM - ROPE_DIM
    wq_plain = jnp.concatenate([uq_n, uq_r, jnp.zeros(uq_n.shape[:2] + (zpad,), F32)], axis=-1)
    wq_swap = jnp.concatenate([jnp.zeros_like(uq_n), swap(uq_r), jnp.zeros(uq_n.shape[:2] + (zpad,), F32)], axis=-1)
    wq2 = jnp.concatenate([wq_plain.reshape(Q_LORA, -1), wq_swap.reshape(Q_LORA, -1)], axis=1)
    wk_pad = pad_to(w_uk, HEAD_PAD)
    wukt = jnp.transpose(wk_pad, (1, 2, 0)).reshape(N_QA, KV_LORA)
    wuv = w_uv.reshape(KV_LORA, W_A)
    r = np.arange(ROPE_DIM)
    d = np.arange(DH_B)
    er = np.zeros((N_QA, LANES), np.float32)
    cmp_ = np.zeros((N_QA, W_B), np.float32)
    pc = np.zeros((N_QA, _N_PARTS), np.float32)
    patq = np.zeros((1, N_QA), np.float32)
    patv = np.zeros((1, N_QA), np.float32)
    for hd in range(N_HEADS):
        g0 = hd * HEAD_PAD
        er[g0 + NOPE_DIM + r, r] = 1.0
        cmp_[g0 + d, hd * DH_B + d] = 1.0
        for k in range(N_SPLIT):
            pc[g0 + DH_B + k, k * N_HEADS + hd] = -1.0
            patq[0, g0 + DH_B + k] = 1.0
        patv[0, g0 + (ONE_EVEN if hd % 2 == 0 else ONE_ODD)] = 1.0
    return {
        "g_pre": g_pre.reshape(1, -1), "w1": w1.astype(BF16), "wt": wt.astype(BF16),
        "g_q": g_q.reshape(1, -1), "wq2": wq2.astype(BF16),
        "g_kv": g_kv.reshape(1, -1), "g_kv_col": g_kv.reshape(-1, 1), "b_f": b_f.reshape(N_HEADS, 1),
        "w_pa": w_pa.astype(BF16), "w_pb": w_pb.astype(BF16), "w_out": w_out.astype(BF16),
        "g_post": g_post.reshape(1, -1),
        "wukt": wukt.astype(BF16), "er": jnp.asarray(er, BF16), "cmp": jnp.asarray(cmp_, BF16),
        "wuv": wuv.astype(BF16), "wuvp": value_groups(wuv).astype(BF16),
        "pc": jnp.asarray(pc, BF16), "patq": jnp.asarray(patq), "patv": jnp.asarray(patv),
    }


def _rope_tables(pos, q_scale):
    inv = ROPE_THETA ** (-jnp.arange(0, ROPE_DIM, 2, dtype=F32) / ROPE_DIM)
    ang = pos.astype(F32)[:, None] * inv[None, :]
    cos, sin = jnp.cos(ang), jnp.sin(ang)
    cos2 = jnp.concatenate([cos, cos], axis=1)
    sin2 = jnp.concatenate([-sin, sin], axis=1)
    t = pos.shape[0]
    z = lambda n: jnp.zeros((t, n), F32)
    zpad = HEAD_PAD - NOPE_DIM - ROPE_DIM
    ct = jnp.concatenate([jnp.ones((t, NOPE_DIM), F32), cos2, z(zpad)], axis=1) * q_scale
    st = jnp.concatenate([z(NOPE_DIM), sin2, z(zpad)], axis=1) * q_scale
    kt = jnp.concatenate([cos2, z(LANES - ROPE_DIM), sin2, z(LANES - ROPE_DIM)], axis=1).T
    return jnp.concatenate([ct, st], axis=1), kt


def _pick(n, prefs):
    for b in prefs:
        if n % b == 0:
            return b
    return n


def _layer(x_p, x_s, page_table, c_lat, c_kr, c_k, c_v, c_lf, pw):
    bp, sp, _ = x_p.shape
    bs, ss, _ = x_s.shape
    assert bp == 1 and ss == 1
    n_pool = c_lat.shape[0]
    past = page_table.shape[1] * PAGE_SIZE
    xp = x_p.reshape(sp, D_MODEL)
    xs = x_s.reshape(bs, D_MODEL)

    (lat, krt, kbt, vbt, lf_t, qa, kat, va, qb, kbt16, vb16, gates) = _project(
        xp, pw, *_rope_tables(jnp.arange(sp), MLA_SCALE * LOG2E), _pick(sp, (256, 128)),
        FOX_SCALE * LOG2E, LOG2E)
    bq = _pick(sp, (FLASH_BQ, 512, 256, 128))
    oa, ob = _flash_prompt(qa, kat, va, qb, kbt16, vb16, bq, min(FLASH_BK, bq))
    y_p = _merge(xp, oa, ob, gates, pw, _pick(sp, (512, 256, 128)))

    (lat_s, krt_s, kbt_s, vbt_s, lf_ts, qa_s, _, _, qb_s, _, _, gates_s) = _project(
        xs, pw, *_rope_tables(past + jnp.zeros((bs,), jnp.int32), MLA_SCALE), _pick(bs, (256, 128)),
        FOX_SCALE, 1.0)
    c_krt = jnp.transpose(c_kr, (0, 2, 1))
    c_kt = jnp.transpose(c_k, (0, 2, 3, 1)).reshape(n_pool, W_B, PAGE_SIZE)
    c_vt = jnp.transpose(c_v, (0, 2, 3, 1)).reshape(n_pool, W_B, PAGE_SIZE)
    c_lft = jnp.transpose(c_lf, (0, 2, 1))
    oa_s, ob_s = _decode_sample(page_table, qa_s, qb_s, lat_s, krt_s, kbt_s, vbt_s, lf_ts, pw,
                                c_lat, c_krt, c_kt, c_vt, c_lft, _pick(page_table.shape[1] // 2, (DECODE_PAGES, 8, 4, 2, 1)))
    y_s = _merge(xs, oa_s, ob_s, gates_s, pw, _pick(bs, (512, 256, 128)))

    heads = lambda a, n: jnp.transpose(a.reshape(N_HEADS, DH_B, n), (2, 0, 1))
    new_p = (lat.reshape(bp, sp, KV_LORA), krt.T.reshape(bp, sp, ROPE_DIM), heads(kbt, sp).reshape(bp, sp, N_HEADS, DH_B),
             heads(vbt, sp).reshape(bp, sp, N_HEADS, DH_B), lf_t.T.reshape(bp, sp, N_HEADS))
    new_s = (lat_s.reshape(bs, ss, KV_LORA), krt_s.T.reshape(bs, ss, ROPE_DIM),
             heads(kbt_s, bs).reshape(bs, ss, N_HEADS, DH_B), heads(vbt_s, bs).reshape(bs, ss, N_HEADS, DH_B),
             lf_ts.T.reshape(bs, ss, N_HEADS))
    return y_p.reshape(bp, sp, D_MODEL), y_s.reshape(bs, ss, D_MODEL), new_p, new_s


def kernel(x_prompt, x_sample, cache_mla_latent, cache_mla_krope, cache_fox_k, cache_fox_v, cache_fox_logf,
           page_table, g_pre, w_in, g_q, w_uq, g_kv, w_uk, w_uv, b_f, w_pa, w_pb, w_out, g_post):
    depth = w_in.shape[0]
    x_p, x_s = x_prompt, x_sample
    st_p = [[] for _ in range(5)]
    st_s = [[] for _ in range(5)]
    for l in range(depth):
        pw = _prep_weights(g_pre[l], w_in[l], g_q[l], w_uq[l], g_kv[l], w_uk[l], w_uv[l], b_f[l],
                           w_pa[l], w_pb[l], w_out[l], g_post[l])
        lay = lambda a: a.reshape(a.shape[1:]) if depth == 1 else a[l]
        x_p, x_s, new_p, new_s = _layer(x_p, x_s, page_table, lay(cache_mla_latent), lay(cache_mla_krope),
                                        lay(cache_fox_k), lay(cache_fox_v), lay(cache_fox_logf), pw)
        for lst, a in zip(st_p, new_p):
            lst.append(a)
        for lst, a in zip(st_s, new_s):
            lst.append(a)
    return (x_p, x_s) + tuple(jnp.stack(s) for s in st_p) + tuple(jnp.stack(s) for s in st_s)
```

```python
import functools
import math

import numpy as np
import jax
import jax.numpy as jnp
from jax import lax
from jax.experimental import pallas as pl
from jax.experimental.pallas import tpu as pltpu

D_MODEL = 1024
N_HEADS = 8
Q_LORA = 256
KV_LORA = 256
NOPE_DIM = 64
ROPE_DIM = 32
V_DIM = 64
DH_B = 64
W_A = N_HEADS * V_DIM
W_B = N_HEADS * DH_B
PAGE_SIZE = 128
ROPE_THETA = 10000.0
EPS = 1e-6
MLA_SCALE = (NOPE_DIM + ROPE_DIM) ** -0.5
FOX_SCALE = DH_B ** -0.5
LOG2E = math.log2(math.e)
LANES = 128
HEAD_PAD = 128
N_QA = N_HEADS * HEAD_PAD
N_SPLIT = 3
ONE_EVEN = V_DIM
ONE_ODD = 0

_C_CQ = 0
_C_CKV = _C_CQ + Q_LORA
_C_QB = _C_CKV + KV_LORA
_C_VB = _C_QB + N_QA
_N_W1 = _C_VB + N_QA
_N_GATES = W_A + W_B + 2 * D_MODEL
_R_KB = 0
_R_VB = _R_KB + N_QA
_R_CKV = _R_VB + W_B
_R_KR = _R_CKV + KV_LORA
_R_F = _R_KR + 2 * LANES
_N_WT = _R_F + 16
_N_PARTS = 32

VMEM_LIMIT = 56 * 1024 * 1024
PROJECT_BM = 512
FLASH_BQ = 1024
FLASH_BK = 512
DECODE_PAGES = 16

BF16 = jnp.bfloat16
F32 = jnp.float32


def _dot(a, b):
    return jnp.dot(a, b, preferred_element_type=F32)


def _dot_nt(a, b):
    return lax.dot_general(a, b, (((1,), (1,)), ((), ())), preferred_element_type=F32)


def _rms(x, g):
    return x * lax.rsqrt(jnp.mean(x * x, axis=-1, keepdims=True) + EPS) * g


def _sigmoid(x):
    return 1.0 / (1.0 + jnp.exp(-x))


def _split3(x):
    x1 = x.astype(BF16).astype(F32)
    r1 = x - x1
    x2 = r1.astype(BF16).astype(F32)
    x3 = (r1 - x2).astype(BF16).astype(F32)
    return x1, x2, x3


def _project_kernel(x_ref, gpre_ref, w1_ref, wt_ref, gq_ref, wq2_ref, gkv_ref, gkvc_ref, wuvp_ref, wukt_ref, er_ref,
                    bf_ref, u_ref, pc_ref, patq_ref, patv_ref, tab_ref, tabt_ref,
                    lat_ref, krt_ref, kbt_ref, vbt_ref, lf_ref, qa_ref, kat_ref, va_ref, qb_ref,
                    kbt16_ref, vb16_ref, carry_ref, *, qb_scale, cum_scale):
    i = pl.program_id(0)

    @pl.when(i == 0)
    def _():
        carry_ref[...] = jnp.zeros_like(carry_ref)

    x = x_ref[...]
    h = _rms(x, gpre_ref[...]).astype(BF16)

    cq = _dot(h, w1_ref[:, _C_CQ:_C_CQ + Q_LORA])
    cqn = _rms(cq, gq_ref[...]).astype(BF16)
    q2 = _dot(cqn, wq2_ref[...])
    ct = tab_ref[:, 0:LANES]
    st = tab_ref[:, LANES:]
    for hd in range(N_HEADS):
        lo = hd * HEAD_PAD
        qa_ref[:, lo:lo + HEAD_PAD] = (q2[:, lo:lo + HEAD_PAD] * ct
                                       + q2[:, N_QA + lo:N_QA + lo + HEAD_PAD] * st).astype(BF16)

    ckv = _dot(h, w1_ref[:, _C_CKV:_C_CKV + KV_LORA])
    lat = _rms(ckv, gkv_ref[...])
    lat_ref[...] = lat
    va_ref[...] = (_dot(lat.astype(BF16), wuvp_ref[...]) + patv_ref[...]).astype(BF16)

    zt = _dot_nt(wt_ref[...], h)
    kbz = zt[_R_KB:_R_KB + N_QA]
    for hd in range(N_HEADS):
        kbt_ref[hd * DH_B:(hd + 1) * DH_B, :] = kbz[hd * HEAD_PAD:hd * HEAD_PAD + DH_B]
    vbt_ref[...] = zt[_R_VB:_R_VB + W_B]
    ckvt = zt[_R_CKV:_R_CKV + KV_LORA]
    latt = ckvt * lax.rsqrt(jnp.mean(ckvt * ckvt, axis=0, keepdims=True) + EPS) * gkvc_ref[...]
    krt = (zt[_R_KR:_R_KR + LANES] * tabt_ref[0:LANES, :]
           + zt[_R_KR + LANES:_R_KR + 2 * LANES] * tabt_ref[LANES:, :])
    krt_ref[...] = krt[0:ROPE_DIM]
    kat_ref[...] = (_dot(wukt_ref[...], latt.astype(BF16)) + _dot(er_ref[...], krt.astype(BF16))).astype(BF16)

    qb_ref[...] = (_dot(h, w1_ref[:, _C_QB:_C_QB + N_QA]) * qb_scale + patq_ref[...]).astype(BF16)
    vb16_ref[...] = (_dot(h, w1_ref[:, _C_VB:_C_VB + N_QA]) + patv_ref[...]).astype(BF16)

    f_t = zt[_R_F:_R_F + N_HEADS] + bf_ref[...]
    lf = jnp.minimum(f_t, 0.0) - jnp.log(1.0 + jnp.exp(-jnp.abs(f_t)))
    lf_ref[...] = lf
    parts = jnp.concatenate(_split3(lf), axis=0).astype(BF16)
    cs = _dot(parts, u_ref[...])
    cum = cs[0:N_HEADS] + cs[N_HEADS:2 * N_HEADS] + cs[2 * N_HEADS:] + carry_ref[:, 0:1]
    bm = cum.shape[1]
    carry_ref[...] = jnp.broadcast_to(cum[:, bm - 1:bm], carry_ref.shape)
    cparts = jnp.concatenate(_split3(cum * cum_scale) + (jnp.zeros_like(cum),), axis=0).astype(BF16)
    kbt16_ref[...] = (kbz + _dot(pc_ref[...], cparts)).astype(BF16)


def _project(x, pw, tab, tabt, bm, qb_scale, cum_scale):
    n = x.shape[0]
    assert n % bm == 0
    grid = (n // bm,)
    row = lambda w: pl.BlockSpec((bm, w), lambda i: (i, 0))
    col = lambda r: pl.BlockSpec((r, bm), lambda i: (0, i))
    const = lambda a: pl.BlockSpec(a.shape, lambda i: (0,) * a.ndim, pipeline_mode=pl.Buffered(1))
    u = (jnp.arange(bm)[:, None] <= jnp.arange(bm)[None, :]).astype(BF16)
    consts = (pw["g_pre"], pw["w1"], pw["wt"], pw["g_q"], pw["wq2"], pw["g_kv"], pw["g_kv_col"], pw["wuvp"],
              pw["wukt"], pw["er"], pw["b_f"], u, pw["pc"], pw["patq"], pw["patv"])
    out_shape = (
        jax.ShapeDtypeStruct((n, KV_LORA), F32),
        jax.ShapeDtypeStruct((ROPE_DIM, n), F32),
        jax.ShapeDtypeStruct((W_B, n), F32),
        jax.ShapeDtypeStruct((W_B, n), F32),
        jax.ShapeDtypeStruct((N_HEADS, n), F32),
        jax.ShapeDtypeStruct((n, N_QA), BF16),
        jax.ShapeDtypeStruct((N_QA, n), BF16),
        jax.ShapeDtypeStruct((n, N_QA), BF16),
        jax.ShapeDtypeStruct((n, N_QA), BF16),
        jax.ShapeDtypeStruct((N_QA, n), BF16),
        jax.ShapeDtypeStruct((n, N_QA), BF16),
    )
    out_specs = (row(KV_LORA), col(ROPE_DIM), col(W_B), col(W_B), col(N_HEADS),
                 row(N_QA), col(N_QA), row(N_QA), row(N_QA), col(N_QA), row(N_QA))
    return pl.pallas_call(
        functools.partial(_project_kernel, qb_scale=qb_scale, cum_scale=cum_scale),
        grid=grid,
        in_specs=[row(D_MODEL)] + [const(a) for a in consts] + [row(2 * LANES), col(2 * LANES)],
        out_specs=out_specs,
        out_shape=out_shape,
        scratch_shapes=[pltpu.VMEM((N_HEADS, LANES), F32)],
        compiler_params=pltpu.CompilerParams(dimension_semantics=("arbitrary",), vmem_limit_bytes=VMEM_LIMIT),
        name="project",
    )(x, *consts, tab, tabt)


def _flash_kernel(qa_ref, kat_ref, va_ref, qb_ref, kbt_ref, vb_ref, oa_ref, ob_ref, m_ref, acc_ref, *, bk):
    qi = pl.program_id(1)
    blk = qa_ref.shape[0]
    sub = blk // bk
    m_ref[...] = jnp.full_like(m_ref, -jnp.inf)
    acc_ref[...] = jnp.zeros_like(acc_ref)

    def block(k0, diag):
        if diag is not None:
            keep = (lax.broadcasted_iota(jnp.int32, (blk, bk), 1) + diag * bk
                    <= lax.broadcasted_iota(jnp.int32, (blk, bk), 0))
        scores = {}
        for hh in range(4):
            q_ref, k_ref, v_ref = (qa_ref, kat_ref, va_ref) if hh < 2 else (qb_ref, kbt_ref, vb_ref)
            lo = (hh % 2) * HEAD_PAD
            if hh % 2 == 0:
                for h2 in (hh, hh + 1):
                    l2 = (h2 % 2) * HEAD_PAD
                    scores[h2] = _dot(q_ref[:, l2:l2 + HEAD_PAD], k_ref[l2:l2 + HEAD_PAD, pl.ds(k0, bk)])
            s = scores[hh]
            if diag is not None:
                s = jnp.where(keep, s, -jnp.inf)
            m_prev = m_ref[hh]
            m_new = jnp.maximum(m_prev, jnp.max(s, axis=1, keepdims=True))
            alpha = jnp.exp2(m_prev - m_new)
            p = jnp.exp2(s - m_new[:, 0:1]).astype(BF16)
            acc_ref[hh] = alpha * acc_ref[hh] + _dot(p, v_ref[pl.ds(k0, bk), lo:lo + HEAD_PAD])
            m_ref[hh] = m_new

    def key_blocks(ki, c):
        def visible():
            for t in range(sub):
                block(pl.multiple_of(ki * blk + t * bk, bk), None)

        def diagonal():
            for t in range(sub):
                block(pl.multiple_of(ki * blk + t * bk, bk), t)

        lax.cond(ki == qi, diagonal, visible)
        return c

    lax.fori_loop(0, qi + 1, key_blocks, 0)

    half = lax.broadcasted_iota(jnp.int32, (blk, LANES), 1) < V_DIM

    def pair_out(a_even, a_odd):
        return jnp.where(half, a_even / a_even[:, ONE_EVEN:ONE_EVEN + 1], a_odd / a_odd[:, ONE_ODD:ONE_ODD + 1])

    oa_ref[...] = pair_out(acc_ref[0], acc_ref[1])
    ob_ref[...] = pair_out(acc_ref[2], acc_ref[3])


def _flash_prompt(qa, kat, va, qb, kbt, vb, blk, bk):
    n = qa.shape[0]
    assert n % blk == 0 and blk % bk == 0
    n_pairs = N_HEADS // 2
    qmap = pl.BlockSpec((blk, 2 * HEAD_PAD), lambda p, i: (i, p))
    kmap = pl.BlockSpec((2 * HEAD_PAD, n), lambda p, i: (p, 0), pipeline_mode=pl.Buffered(1))
    vmap = pl.BlockSpec((n, 2 * HEAD_PAD), lambda p, i: (0, p), pipeline_mode=pl.Buffered(1))
    omap = pl.BlockSpec((blk, LANES), lambda p, i: (i, p))
    return pl.pallas_call(
        functools.partial(_flash_kernel, bk=bk),
        grid=(n_pairs, n // blk),
        in_specs=[qmap, kmap, vmap, qmap, kmap, vmap],
        out_specs=(omap, omap),
        out_shape=(jax.ShapeDtypeStruct((n, W_A), F32), jax.ShapeDtypeStruct((n, W_B), F32)),
        scratch_shapes=[pltpu.VMEM((4, blk, LANES), F32), pltpu.VMEM((4, blk, LANES), F32)],
        compiler_params=pltpu.CompilerParams(dimension_semantics=("arbitrary", "arbitrary"),
                                             vmem_limit_bytes=VMEM_LIMIT),
        name="flash_prompt",
    )(qa, kat, va, qb, kbt, vb)


def _merge_kernel(x_ref, oa_ref, ob_ref, gpre_ref, wg_ref, wpa_ref, wpb_ref, wout_ref, gpost_ref, y_ref):
    x = x_ref[...]
    h = _rms(x, gpre_ref[...]).astype(BF16)
    ga = _dot(h, wg_ref[:, 0:W_A])
    ua = (oa_ref[...] * (ga * _sigmoid(ga))).astype(BF16)
    gb = _dot(h, wg_ref[:, W_A:W_A + W_B])
    ub = (ob_ref[...] * (gb * _sigmoid(gb))).astype(BF16)
    gla = _dot(h, wg_ref[:, W_A + W_B:W_A + W_B + D_MODEL])
    glb = _dot(h, wg_ref[:, W_A + W_B + D_MODEL:])
    m = _sigmoid(gla) * _dot(ua, wpa_ref[...]) + _sigmoid(glb) * _dot(ub, wpb_ref[...])
    mo = _dot(m.astype(BF16), wout_ref[...])
    y_ref[...] = x + _rms(mo, gpost_ref[...])


def _merge(x, oa, ob, pw, bm):
    n = x.shape[0]
    assert n % bm == 0
    row = lambda w: pl.BlockSpec((bm, w), lambda i: (i, 0))
    const = lambda a: pl.BlockSpec(a.shape, lambda i: (0,) * a.ndim, pipeline_mode=pl.Buffered(1))
    consts = (pw["g_pre"], pw["wg"], pw["w_pa"], pw["w_pb"], pw["w_out"], pw["g_post"])
    return pl.pallas_call(
        _merge_kernel,
        grid=(n // bm,),
        in_specs=[row(D_MODEL), row(W_A), row(W_B)] + [const(a) for a in consts],
        out_specs=row(D_MODEL),
        out_shape=jax.ShapeDtypeStruct((n, D_MODEL), F32),
        compiler_params=pltpu.CompilerParams(dimension_semantics=("parallel",), vmem_limit_bytes=VMEM_LIMIT),
        name="merge",
    )(x, oa, ob, *consts)


def _bf(x):
    return x.astype(BF16).astype(F32)


def _decode_kernel(pt_ref, qa_ref, qb_ref, latn_ref, krn_ref, kn_ref, vn_ref, lfn_ref, wukt_ref, er_ref, cmp_ref,
                   wuv_ref, tt_ref, clat_hbm, ckr_hbm, ck_hbm, cv_hbm, clf_hbm, oa_ref, ob_ref,
                   lat_buf, kr_buf, k_buf, v_buf, lf_buf, sem,
                   qlat_ref, qrope_ref, qbd_ref, ma_ref, la_ref, mb_ref, lb_ref, acca_ref, accb_ref, carry_ref,
                   pb_ref, alpha_ref, *, pages_per_step):
    pps = pages_per_step
    b = pl.program_id(0)
    nb = pl.num_programs(0)
    n_pages = pt_ref.shape[1]
    n_steps = n_pages // pps
    n_pairs = n_steps // 2
    caches = (clat_hbm, ckr_hbm, ck_hbm, cv_hbm, clf_hbm)
    bufs = (lat_buf, kr_buf, k_buf, v_buf, lf_buf)

    def page_copy(a, page, slot, i):
        return pltpu.make_async_copy(caches[a].at[page], bufs[a].at[slot, i], sem.at[slot, a])

    def start_step(bb, jj, slot):
        for i in range(pps):
            page = pt_ref[bb, n_pages - 1 - (jj * pps + i)]
            for a in range(len(caches)):
                page_copy(a, page, slot, i).start()

    def wait_step(slot):
        for i in range(pps):
            for a in range(len(caches)):
                page_copy(a, 0, slot, i).wait()

    def head_mask(width, group):
        lane = lax.broadcasted_iota(jnp.int32, (N_HEADS, width), 1)
        sub = lax.broadcasted_iota(jnp.int32, (N_HEADS, width), 0)
        return (lane // group) == sub

    @pl.when(b == 0)
    def _():
        start_step(0, 0, 0)

    def init_state():
        mine = lax.broadcasted_iota(jnp.int32, (1, LANES), 1) == lax.rem(b, LANES)

        def pick(x8):
            return jnp.sum(jnp.where(mine, x8, 0.0), axis=1, keepdims=True)

        groups = head_mask(N_QA, HEAD_PAD)
        qa = jnp.broadcast_to(qa_ref[...].astype(F32), (N_HEADS, N_QA))
        qa = jnp.where(groups, qa, 0.0).astype(BF16)
        qlat = _dot(qa, wukt_ref[...])
        qrope = _dot(qa, er_ref[...])
        qb = jnp.broadcast_to(qb_ref[...].astype(F32), (N_HEADS, N_QA))
        qbd = _dot(jnp.where(groups, qb, 0.0).astype(BF16), cmp_ref[...])
        qlat_ref[...] = qlat
        qrope_ref[...] = qrope
        qbd_ref[...] = qbd
        latn = _bf(latn_ref[...])
        sa = (jnp.sum(_bf(qlat) * latn, axis=1, keepdims=True)
              + pick(_dot(qrope[:, 0:ROPE_DIM].astype(BF16), krn_ref[...].astype(BF16))))
        sb = pick(_dot(qbd.astype(BF16), kn_ref[...].astype(BF16)))
        ma_ref[...] = jnp.broadcast_to(sa, ma_ref.shape)
        mb_ref[...] = jnp.broadcast_to(sb, mb_ref.shape)
        la_ref[...] = jnp.ones_like(la_ref)
        lb_ref[...] = jnp.ones_like(lb_ref)
        acca_ref[...] = jnp.broadcast_to(latn, acca_ref.shape)
        accb_ref[...] = jnp.where(mine, vn_ref[...], 0.0)
        carry_ref[...] = jnp.broadcast_to(pick(lfn_ref[...]), carry_ref.shape)

    def update(s, m_ref, l_ref):
        m_prev = m_ref[...]
        m_new = jnp.maximum(m_prev, jnp.max(s, axis=1, keepdims=True))
        alpha = jnp.exp(m_prev - m_new)
        p = jnp.exp(s - m_new[:, 0:1])
        l_ref[...] = alpha * l_ref[...] + jnp.sum(p, axis=1, keepdims=True)
        m_ref[...] = m_new
        return alpha, p

    def attend(slot):
        qlat = qlat_ref[...].astype(BF16)
        qrope = qrope_ref[:, 0:ROPE_DIM].astype(BF16)
        qbd = qbd_ref[...].astype(BF16)
        lats = [lat_buf[slot, i].astype(BF16) for i in range(pps)]
        carry = carry_ref[...]
        sa_parts, sb_parts = [], []
        for i in range(pps):
            parts = jnp.concatenate(_split3(lf_buf[slot, i]), axis=0).astype(BF16)
            y3 = _dot(parts, tt_ref[...])
            y = y3[0:N_HEADS] + y3[N_HEADS:2 * N_HEADS] + y3[2 * N_HEADS:]
            sb_parts.append(_dot(qbd, k_buf[slot, i].astype(BF16)) + y[:, 0:PAGE_SIZE] + carry)
            carry = carry + y[:, PAGE_SIZE:]
            sa_parts.append(_dot_nt(qlat, lats[i]) + _dot(qrope, kr_buf[slot, i].astype(BF16)))
        carry_ref[...] = carry
        alpha_a, pa = update(jnp.concatenate(sa_parts, axis=1), ma_ref, la_ref)
        alpha_b, pb = update(jnp.concatenate(sb_parts, axis=1), mb_ref, lb_ref)
        pa = pa.astype(BF16)
        da = _dot(pa[:, 0:PAGE_SIZE], lats[0])
        for i in range(1, pps):
            da = da + _dot(pa[:, i * PAGE_SIZE:(i + 1) * PAGE_SIZE], lats[i])
        acca_ref[...] = alpha_a[:, 0:1] * acca_ref[...] + da
        pb_ref[...] = pb
        alpha_ref[...] = alpha_b
        for hd in range(N_HEADS):
            rows = slice(hd * DH_B, (hd + 1) * DH_B)
            acc = accb_ref[rows, :] * alpha_ref[hd:hd + 1, :]
            for i in range(pps):
                acc = acc + v_buf[slot, i, rows, :] * pb_ref[hd:hd + 1, i * PAGE_SIZE:(i + 1) * PAGE_SIZE]
            accb_ref[rows, :] = acc

    init_state()

    def two_steps(pair, c):
        j0 = 2 * pair
        start_step(b, j0 + 1, 1)
        wait_step(0)
        attend(0)
        last = pair == n_pairs - 1
        nxt_b = jnp.where(last, jnp.minimum(b + 1, nb - 1), b)
        nxt_j = jnp.where(last, jnp.where(b + 1 < nb, 0, n_steps - 1), j0 + 2)
        start_step(nxt_b, nxt_j, 0)
        wait_step(1)
        attend(1)
        return c

    lax.fori_loop(0, n_pairs, two_steps, 0)

    @pl.when(b == nb - 1)
    def _():
        wait_step(0)

    olat = (acca_ref[...] / la_ref[:, 0:1]).astype(BF16)
    oa8 = _dot(olat, wuv_ref[...])
    diag = head_mask(W_A, V_DIM)
    oa_ref[...] = jnp.sum(jnp.where(diag, oa8, 0.0), axis=0, keepdims=True)
    ones = jnp.ones((N_HEADS, LANES), BF16)
    ob8 = sum(_dot_nt(ones, part.astype(BF16)) for part in _split3(accb_ref[...]))
    ob_ref[...] = jnp.sum(jnp.where(diag, ob8 / lb_ref[:, 0:1], 0.0), axis=0, keepdims=True)


def _decode_sample(page_table, qa, qb, lat_new, krt_new, kbt_new, vbt_new, lft_new, pw, c_lat, c_krt, c_kt, c_vt,
                   c_lft, pages_per_step):
    nb, n_pages = page_table.shape
    pps = pages_per_step
    assert n_pages % (2 * pps) == 0 and nb % LANES == 0
    per_b = lambda w: pl.BlockSpec((None, 1, w), lambda b, pt: (b, 0, 0))
    lane_b = lambda r: pl.BlockSpec((r, LANES), lambda b, pt: (0, b // LANES))
    const = lambda a: pl.BlockSpec(a.shape, lambda b, pt: (0,) * a.ndim)
    in_hbm = pl.BlockSpec(memory_space=pl.ANY)
    pos = np.arange(PAGE_SIZE)
    tt = jnp.asarray(np.concatenate([pos[:, None] > pos[None, :], np.ones((PAGE_SIZE, PAGE_SIZE), bool)], axis=1),
                     BF16)
    consts = (pw["wukt"], pw["er"], pw["cmp"], pw["wuv"], tt)
    caches = (c_lat, c_krt, c_kt, c_vt, c_lft)
    small = lambda w: pltpu.VMEM((N_HEADS, w), F32)
    grid_spec = pltpu.PrefetchScalarGridSpec(
        num_scalar_prefetch=1,
        grid=(nb,),
        in_specs=([per_b(N_QA), per_b(N_QA), per_b(KV_LORA), lane_b(ROPE_DIM), lane_b(W_B), lane_b(W_B),
                   lane_b(N_HEADS)] + [const(a) for a in consts] + [in_hbm] * len(caches)),
        out_specs=(per_b(W_A), per_b(W_B)),
        scratch_shapes=([pltpu.VMEM((2, pps) + c.shape[1:], F32) for c in caches]
                        + [pltpu.SemaphoreType.DMA((2, len(caches)))]
                        + [small(KV_LORA), small(LANES), small(W_B),
                           small(LANES), small(LANES), small(LANES), small(LANES),
                           small(KV_LORA), pltpu.VMEM((W_B, LANES), F32), small(LANES),
                           small(pps * PAGE_SIZE), small(LANES)]),
    )
    r3 = lambda a: a.reshape(nb, 1, a.shape[-1])
    oa, ob = pl.pallas_call(
        functools.partial(_decode_kernel, pages_per_step=pps),
        grid_spec=grid_spec,
        out_shape=(jax.ShapeDtypeStruct((nb, 1, W_A), F32), jax.ShapeDtypeStruct((nb, 1, W_B), F32)),
        compiler_params=pltpu.CompilerParams(dimension_semantics=("arbitrary",), vmem_limit_bytes=VMEM_LIMIT),
        name="decode_sample",
    )(page_table, r3(qa), r3(qb), r3(lat_new), krt_new, kbt_new, vbt_new, lft_new, *consts, *caches)
    return oa.reshape(nb, W_A), ob.reshape(nb, W_B)


def _head_groups(w, offset=0):
    lead = w.shape[:-1]
    w = w.reshape(lead + (N_HEADS, -1))
    width = w.shape[-1]
    pads = [(0, 0)] * (w.ndim - 1) + [(offset, HEAD_PAD - width - offset)]
    return jnp.pad(w, pads).reshape(lead + (N_QA,))


def _prep_weights(g_pre, w_in, g_q, w_uq, g_kv, w_uk, w_uv, b_f, w_pa, w_pb, w_out, g_post):
    offs = np.cumsum([0, Q_LORA, KV_LORA, ROPE_DIM, W_A, W_B, W_B, W_B, N_HEADS, W_B, D_MODEL, D_MODEL])
    seg = lambda k: w_in[:, offs[k]:offs[k + 1]]
    half = ROPE_DIM // 2
    swap = lambda w: jnp.concatenate([w[..., half:], w[..., :half]], axis=-1)
    pad_to = lambda w, n: jnp.pad(w, [(0, 0)] * (w.ndim - 1) + [(0, n - w.shape[-1])])
    even = (np.arange(N_HEADS) % 2 == 0)

    def value_groups(w):
        lo, hi = _head_groups(w, 0), _head_groups(w, HEAD_PAD - V_DIM)
        sel = jnp.asarray(np.repeat(even, HEAD_PAD))
        return jnp.where(sel, lo, hi)

    w_kr = seg(2)
    w1 = jnp.concatenate([seg(0), seg(1), _head_groups(seg(4)), value_groups(seg(6))], axis=1)
    assert w1.shape[1] == _N_W1
    wg = jnp.concatenate([seg(3), seg(8), seg(9), seg(10)], axis=1)
    assert wg.shape[1] == _N_GATES
    wt = jnp.concatenate([_head_groups(seg(5)), seg(6), seg(1), pad_to(w_kr, LANES), pad_to(swap(w_kr), LANES),
                          pad_to(seg(7), 16)], axis=1).T
    assert wt.shape[0] == _N_WT
    uq_n, uq_r = w_uq[..., :NOPE_DIM], w_uq[..., NOPE_DIM:]
    zpad = HEAD_PAD - NOPE_DIM - ROPE_DIM
    wq_plain = jnp.concatenate([uq_n, uq_r, jnp.zeros(uq_n.shape[:2] + (zpad,), F32)], axis=-1)
    wq_swap = jnp.concatenate([jnp.zeros_like(uq_n), swap(uq_r), jnp.zeros(uq_n.shape[:2] + (zpad,), F32)], axis=-1)
    wq2 = jnp.concatenate([wq_plain.reshape(Q_LORA, -1), wq_swap.reshape(Q_LORA, -1)], axis=1)
    wk_pad = pad_to(w_uk, HEAD_PAD)
    wukt = jnp.transpose(wk_pad, (1, 2, 0)).reshape(N_QA, KV_LORA)
    wuv = w_uv.reshape(KV_LORA, W_A)
    r = np.arange(ROPE_DIM)
    d = np.arange(DH_B)
    er = np.zeros((N_QA, LANES), np.float32)
    cmp_ = np.zeros((N_QA, W_B), np.float32)
    pc = np.zeros((N_QA, _N_PARTS), np.float32)
    patq = np.zeros((1, N_QA), np.float32)
    patv = np.zeros((1, N_QA), np.float32)
    for hd in range(N_HEADS):
        g0 = hd * HEAD_PAD
        er[g0 + NOPE_DIM + r, r] = 1.0
        cmp_[g0 + d, hd * DH_B + d] = 1.0
        for k in range(N_SPLIT):
            pc[g0 + DH_B + k, k * N_HEADS + hd] = -1.0
            patq[0, g0 + DH_B + k] = 1.0
        patv[0, g0 + (ONE_EVEN if hd % 2 == 0 else ONE_ODD)] = 1.0
    return {
        "g_pre": g_pre.reshape(1, -1), "w1": w1.astype(BF16), "wt": wt.astype(BF16), "wg": wg.astype(BF16),
        "g_q": g_q.reshape(1, -1), "wq2": wq2.astype(BF16),
        "g_kv": g_kv.reshape(1, -1), "g_kv_col": g_kv.reshape(-1, 1), "b_f": b_f.reshape(N_HEADS, 1),
        "w_pa": w_pa.astype(BF16), "w_pb": w_pb.astype(BF16), "w_out": w_out.astype(BF16),
        "g_post": g_post.reshape(1, -1),
        "wukt": wukt.astype(BF16), "er": jnp.asarray(er, BF16), "cmp": jnp.asarray(cmp_, BF16),
        "wuv": wuv.astype(BF16), "wuvp": value_groups(wuv).astype(BF16),
        "pc": jnp.asarray(pc, BF16), "patq": jnp.asarray(patq), "patv": jnp.asarray(patv),
    }


def _rope_tables(pos, q_scale):
    inv = ROPE_THETA ** (-jnp.arange(0, ROPE_DIM, 2, dtype=F32) / ROPE_DIM)
    ang = pos.astype(F32)[:, None] * inv[None, :]
    cos, sin = jnp.cos(ang), jnp.sin(ang)
    cos2 = jnp.concatenate([cos, cos], axis=1)
    sin2 = jnp.concatenate([-sin, sin], axis=1)
    t = pos.shape[0]
    z = lambda n: jnp.zeros((t, n), F32)
    zpad = HEAD_PAD - NOPE_DIM - ROPE_DIM
    ct = jnp.concatenate([jnp.ones((t, NOPE_DIM), F32), cos2, z(zpad)], axis=1) * q_scale
    st = jnp.concatenate([z(NOPE_DIM), sin2, z(zpad)], axis=1) * q_scale
    kt = jnp.concatenate([cos2, z(LANES - ROPE_DIM), sin2, z(LANES - ROPE_DIM)], axis=1).T
    return jnp.concatenate([ct, st], axis=1), kt


def _pick(n, prefs):
    for b in prefs:
        if n % b == 0:
            return b
    return n


def _layer(x_p, x_s, page_table, c_lat, c_kr, c_k, c_v, c_lf, pw):
    bp, sp, _ = x_p.shape
    bs, ss, _ = x_s.shape
    assert bp == 1 and ss == 1
    n_pool = c_lat.shape[0]
    past = page_table.shape[1] * PAGE_SIZE
    xp = x_p.reshape(sp, D_MODEL)
    xs = x_s.reshape(bs, D_MODEL)

    (lat, krt, kbt, vbt, lf_t, qa, kat, va, qb, kbt16, vb16) = _project(
        xp, pw, *_rope_tables(jnp.arange(sp), MLA_SCALE * LOG2E), _pick(sp, (PROJECT_BM, 256, 128)),
        FOX_SCALE * LOG2E, LOG2E)
    bq = _pick(sp, (FLASH_BQ, 512, 256, 128))
    oa, ob = _flash_prompt(qa, kat, va, qb, kbt16, vb16, bq, min(FLASH_BK, bq))
    y_p = _merge(xp, oa, ob, pw, _pick(sp, (512, 256, 128)))

    (lat_s, krt_s, kbt_s, vbt_s, lf_ts, qa_s, _, _, qb_s, _, _) = _project(
        xs, pw, *_rope_tables(past + jnp.zeros((bs,), jnp.int32), MLA_SCALE), _pick(bs, (PROJECT_BM, 256, 128)),
        FOX_SCALE, 1.0)
    c_krt = jnp.transpose(c_kr, (0, 2, 1))
    c_kt = jnp.transpose(c_k, (0, 2, 3, 1)).reshape(n_pool, W_B, PAGE_SIZE)
    c_vt = jnp.transpose(c_v, (0, 2, 3, 1)).reshape(n_pool, W_B, PAGE_SIZE)
    c_lft = jnp.transpose(c_lf, (0, 2, 1))
    oa_s, ob_s = _decode_sample(page_table, qa_s, qb_s, lat_s, krt_s, kbt_s, vbt_s, lf_ts, pw,
                                c_lat, c_krt, c_kt, c_vt, c_lft, _pick(page_table.shape[1] // 2, (DECODE_PAGES, 8, 4, 2, 1)))
    y_s = _merge(xs, oa_s, ob_s, pw, _pick(bs, (512, 256, 128)))

    heads = lambda a, n: jnp.transpose(a.reshape(N_HEADS, DH_B, n), (2, 0, 1))
    new_p = (lat.reshape(bp, sp, KV_LORA), krt.T.reshape(bp, sp, ROPE_DIM), heads(kbt, sp).reshape(bp, sp, N_HEADS, DH_B),
             heads(vbt, sp).reshape(bp, sp, N_HEADS, DH_B), lf_t.T.reshape(bp, sp, N_HEADS))
    new_s = (lat_s.reshape(bs, ss, KV_LORA), krt_s.T.reshape(bs, ss, ROPE_DIM),
             heads(kbt_s, bs).reshape(bs, ss, N_HEADS, DH_B), heads(vbt_s, bs).reshape(bs, ss, N_HEADS, DH_B),
             lf_ts.T.reshape(bs, ss, N_HEADS))
    return y_p.reshape(bp, sp, D_MODEL), y_s.reshape(bs, ss, D_MODEL), new_p, new_s


def kernel(x_prompt, x_sample, cache_mla_latent, cache_mla_krope, cache_fox_k, cache_fox_v, cache_fox_logf,
           page_table, g_pre, w_in, g_q, w_uq, g_kv, w_uk, w_uv, b_f, w_pa, w_pb, w_out, g_post):
    depth = w_in.shape[0]
    x_p, x_s = x_prompt, x_sample
    st_p = [[] for _ in range(5)]
    st_s = [[] for _ in range(5)]
    for l in range(depth):
        pw = _prep_weights(g_pre[l], w_in[l], g_q[l], w_uq[l], g_kv[l], w_uk[l], w_uv[l], b_f[l],
                           w_pa[l], w_pb[l], w_out[l], g_post[l])
        lay = lambda a: a.reshape(a.shape[1:]) if depth == 1 else a[l]
        x_p, x_s, new_p, new_s = _layer(x_p, x_s, page_table, lay(cache_mla_latent), lay(cache_mla_krope),
                                        lay(cache_fox_k), lay(cache_fox_v), lay(cache_fox_logf), pw)
        for lst, a in zip(st_p, new_p):
            lst.append(a)
        for lst, a in zip(st_s, new_s):
            lst.append(a)
    return (x_p, x_s) + tuple(jnp.stack(s) for s in st_p) + tuple(jnp.stack(s) for s in st_s)
```
